```python
import math
import jax, jax.numpy as jnp
from jax import lax
import numpy as np

D_MODEL = 1024
BATCH = 4
SEQ = 4096
DEPTH = 4

PLE_DIM = 256
D_FF = 2816
SB_HEADS = 8
SB_HEAD_DIM = 64
DF_HEADS = 4
DF_QK_DIM = 64
DF_V_DIM = 2 * DF_QK_DIM
SB_W = SB_HEADS * SB_HEAD_DIM
DF_QK_W = DF_HEADS * 2 * DF_QK_DIM
DF_W = DF_HEADS * DF_V_DIM
IN_SPLITS = (SB_W, SB_W, SB_W, DF_QK_W, DF_QK_W, DF_W, D_MODEL, D_MODEL)
D_IN = SB_W * 3 + DF_QK_W * 2 + DF_W + 2 * D_MODEL
BLOCK_Q = 128
ROPE_THETA = 10000.0
NORM_EPS = 1e-6
SUBLN_EPS = 1e-5

kernel_name = "hybrid_stickbreak_diffattn_macaron_trunk"


def rms_norm(x, g, eps=NORM_EPS):
    xf = x.astype(jnp.float32)
    y = xf * lax.rsqrt(jnp.mean(xf * xf, axis=-1, keepdims=True) + eps)
    return (y * g.astype(jnp.float32)).astype(x.dtype)


def swiglu(x, w_gu, w_down):
    g, u = jnp.split(x @ w_gu, 2, axis=-1)
    return (jax.nn.silu(g) * u) @ w_down


def rope(x, positions):
    half = x.shape[-1] // 2
    inv_freq = ROPE_THETA ** (-jnp.arange(half, dtype=jnp.float32) / half)
    ang = positions.astype(jnp.float32)[..., None] * inv_freq
    cos = jnp.cos(ang)[:, :, None, None, :]
    sin = jnp.sin(ang)[:, :, None, None, :]
    x1 = x[..., :half].astype(jnp.float32)
    x2 = x[..., half:].astype(jnp.float32)
    return jnp.concatenate([x1 * cos - x2 * sin, x2 * cos + x1 * sin], axis=-1).astype(x.dtype)


def to_query_blocks(q):
    b, h, s = q.shape[:3]
    qb = q.reshape((b, h, s // BLOCK_Q, BLOCK_Q) + q.shape[3:])
    return jnp.moveaxis(qb, 2, 0)


def from_query_blocks(o):
    nb, b, h, bq, d = o.shape
    return jnp.moveaxis(o, 0, 2).reshape(b, h, nb * bq, d)


def stick_breaking_attention(q, k, v):
    s_len, d = q.shape[2], q.shape[3]
    scale = 1.0 / math.sqrt(d)
    key_pos = jnp.arange(s_len)

    def one_block(args):
        qi, bi = args
        q_pos = bi * BLOCK_Q + jnp.arange(BLOCK_Q)
        strict = key_pos[None, :] < q_pos[:, None]
        z = jnp.einsum('bhqd,bhkd->bhqk', qi, k).astype(jnp.float32) * scale
        log_one_minus = jnp.where(strict, jax.nn.log_sigmoid(-z), 0.0)
        later = lax.cumsum(log_one_minus, axis=3, reverse=True) - log_one_minus
        w = jnp.where(strict, jnp.exp(jax.nn.log_sigmoid(z) + later), 0.0)
        return jnp.einsum('bhqk,bhkd->bhqd', w.astype(v.dtype), v)

    nb = s_len // BLOCK_Q
    out = lax.map(one_block, (to_query_blocks(q), jnp.arange(nb)))
    return from_query_blocks(out)


def differential_attention(q, k, v, lam):
    s_len, d = q.shape[2], q.shape[4]
    scale = 1.0 / math.sqrt(d)
    key_pos = jnp.arange(s_len)

    def one_block(args):
        qi, bi = args
        q_pos = bi * BLOCK_Q + jnp.arange(BLOCK_Q)
        causal = key_pos[None, :] <= q_pos[:, None]
        z = jnp.einsum('bhqcd,bhkcd->bhcqk', qi, k).astype(jnp.float32) * scale
        z = jnp.where(causal, z, -jnp.inf)
        probs = jax.nn.softmax(z, axis=-1)
        a = probs[:, :, 0] - lam * probs[:, :, 1]
        return jnp.einsum('bhqk,bhkd->bhqd', a.astype(v.dtype), v)

    nb = s_len // BLOCK_Q
    out = lax.map(one_block, (to_query_blocks(q), jnp.arange(nb)))
    return from_query_blocks(out)


def setup_inputs(seed: int = 0) -> dict:
    key = jax.random.key(seed)
    ks = jax.random.split(key, 24)
    f32 = jnp.float32

    def w(k, shape, fan_in):
        return jax.random.normal(k, shape, f32) * (fan_in ** -0.5)

    def gain(k, n):
        return 1.0 + 0.05 * jax.random.normal(k, (DEPTH, n), f32)

    offsets = jax.random.randint(ks[2], (BATCH, 1), 0, SEQ, dtype=jnp.int32)
    positions = (jnp.arange(SEQ, dtype=jnp.int32)[None, :] + offsets).astype(jnp.int32)
    return {
        "x": jax.random.normal(ks[0], (BATCH, SEQ, D_MODEL), f32),
        "p": jax.random.normal(ks[1], (DEPTH, BATCH, SEQ, PLE_DIM), f32),
        "positions": positions,
        "ffn1_pre_g": gain(ks[3], D_MODEL),
        "ffn1_w_gu": w(ks[4], (DEPTH, D_MODEL, 2 * D_FF), D_MODEL),
        "ffn1_w_down": w(ks[5], (DEPTH, D_FF, D_MODEL), D_FF),
        "ffn1_post_g": gain(ks[6], D_MODEL),
        "mix_pre_g": gain(ks[7], D_MODEL),
        "w_in": w(ks[8], (DEPTH, D_MODEL, D_IN), D_MODEL),
        "diff_lambda": 0.1 * jax.random.normal(ks[9], (DEPTH, 4, DF_QK_DIM), f32),
        "diff_subln_g": gain(ks[10], DF_V_DIM),
        "w_branch_sb": w(ks[11], (DEPTH, SB_W, D_MODEL), SB_W),
        "w_branch_diff": w(ks[12], (DEPTH, DF_W, D_MODEL), DF_W),
        "w_out": w(ks[13], (DEPTH, D_MODEL, D_MODEL), D_MODEL),
        "mix_post_g": gain(ks[14], D_MODEL),
        "ffn2_pre_g": gain(ks[15], D_MODEL),
        "ffn2_w_gu": w(ks[16], (DEPTH, D_MODEL, 2 * D_FF), D_MODEL),
        "ffn2_w_down": w(ks[17], (DEPTH, D_FF, D_MODEL), D_FF),
        "ffn2_post_g": gain(ks[18], D_MODEL),
        "ple_pre_g": gain(ks[19], D_MODEL),
        "w_ple_gate": w(ks[20], (DEPTH, D_MODEL, D_MODEL), D_MODEL),
        "w_ple_proj": w(ks[21], (DEPTH, PLE_DIM, D_MODEL), PLE_DIM),
        "ple_post_g": gain(ks[22], D_MODEL),
    }


def reference(x, p, positions,
              ffn1_pre_g, ffn1_w_gu, ffn1_w_down, ffn1_post_g,
              mix_pre_g, w_in, diff_lambda, diff_subln_g, w_branch_sb, w_branch_diff, w_out, mix_post_g,
              ffn2_pre_g, ffn2_w_gu, ffn2_w_down, ffn2_post_g,
              ple_pre_g, w_ple_gate, w_ple_proj, ple_post_g):
    b, s, _ = x.shape
    split_at = [int(o) for o in np.cumsum(IN_SPLITS)[:-1]]
    h = x
    for i in range(DEPTH):
        f = swiglu(rms_norm(h, ffn1_pre_g[i]), ffn1_w_gu[i], ffn1_w_down[i])
        h = h + 0.5 * rms_norm(f, ffn1_post_g[i])

        u = rms_norm(h, mix_pre_g[i])
        sb_q, sb_k, sb_v, df_q, df_k, df_v, g_sb, g_df = jnp.split(u @ w_in[i], split_at, axis=-1)

        heads_sb = lambda t: t.reshape(b, s, SB_HEADS, SB_HEAD_DIM).transpose(0, 2, 1, 3)
        o_sb = stick_breaking_attention(heads_sb(sb_q), heads_sb(sb_k), heads_sb(sb_v))
        y_sb = o_sb.transpose(0, 2, 1, 3).reshape(b, s, SB_W) @ w_branch_sb[i]

        lam_init = 0.8 - 0.6 * math.exp(-0.3 * i)
        lp = diff_lambda[i].astype(jnp.float32)
        lam = jnp.exp(jnp.sum(lp[0] * lp[1])) - jnp.exp(jnp.sum(lp[2] * lp[3])) + lam_init
        dq = rope(df_q.reshape(b, s, DF_HEADS, 2, DF_QK_DIM), positions).transpose(0, 2, 1, 3, 4)
        dk = rope(df_k.reshape(b, s, DF_HEADS, 2, DF_QK_DIM), positions).transpose(0, 2, 1, 3, 4)
        dv = df_v.reshape(b, s, DF_HEADS, DF_V_DIM).transpose(0, 2, 1, 3)
        o_df = differential_attention(dq, dk, dv, lam)
        o_df = rms_norm(o_df, diff_subln_g[i], SUBLN_EPS) * (1.0 - lam_init)
        y_df = o_df.transpose(0, 2, 1, 3).reshape(b, s, DF_W) @ w_branch_diff[i]

        merged = jax.nn.sigmoid(g_sb) * y_sb + jax.nn.sigmoid(g_df) * y_df
        h = h + rms_norm(merged @ w_out[i], mix_post_g[i])

        f = swiglu(rms_norm(h, ffn2_pre_g[i]), ffn2_w_gu[i], ffn2_w_down[i])
        h = h + 0.5 * rms_norm(f, ffn2_post_g[i])

        gate = jax.nn.sigmoid(rms_norm(h, ple_pre_g[i]) @ w_ple_gate[i])
        e = (p[i] @ w_ple_proj[i]) * gate
        h = h + rms_norm(e, ple_post_g[i])
    return h
```

```python
import functools
import math

import jax
import jax.numpy as jnp
from jax import lax
from jax.experimental import pallas as pl
from jax.experimental.pallas import tpu as pltpu

NORM_EPS = 1e-6
SUBLN_EPS = 1e-5
ROPE_THETA = 10000.0
SB_HEADS = 8
SB_HEAD_DIM = 64
DF_HEADS = 4
DF_QK_DIM = 64
DF_V_DIM = 2 * DF_QK_DIM
SB_W = SB_HEADS * SB_HEAD_DIM
DF_W = DF_HEADS * DF_V_DIM

LANES = 128
VMEM_LIMIT_BYTES = 48 * 1024 * 1024

F32 = jnp.float32
BF16 = jnp.bfloat16


def _pick_tile(n, target):
    t = min(n, target)
    while n % t:
        t //= 2
    return t


def _rms(x, g, eps):
    return x * lax.rsqrt(jnp.mean(x * x, axis=-1, keepdims=True) + eps) * g


def _dot(a, b):
    return jnp.dot(a, b, preferred_element_type=F32)


def _dot_nt(a, b):
    return lax.dot_general(a, b, (((1,), (1,)), ((), ())), preferred_element_type=F32)


def _params(*sem):
    return pltpu.CompilerParams(dimension_semantics=sem, vmem_limit_bytes=VMEM_LIMIT_BYTES)


def _ffn_kernel(*refs, with_ple):
    if with_ple:
        (h_ref, pre_g_ref, wg_ref, wu_ref, wd_ref, post_g_ref,
         p_ref, ple_pre_g_ref, w_gate_ref, w_proj_ref, ple_post_g_ref,
         out_ref, xn_ref, acc_ref) = refs
    else:
        h_ref, pre_g_ref, wg_ref, wu_ref, wd_ref, post_g_ref, out_ref, xn_ref, acc_ref = refs
    j = pl.program_id(1)

    @pl.when(j == 0)
    def _():
        xn_ref[...] = _rms(h_ref[...], pre_g_ref[...], NORM_EPS).astype(BF16)
        acc_ref[...] = jnp.zeros_like(acc_ref)

    xn = xn_ref[...]
    g = _dot(xn, wg_ref[...])
    u = _dot(xn, wu_ref[...])
    a = (g * jax.nn.sigmoid(g) * u).astype(BF16)
    acc_ref[...] += _dot(a, wd_ref[...])

    @pl.when(j == pl.num_programs(1) - 1)
    def _():
        h = h_ref[...] + 0.5 * _rms(acc_ref[...], post_g_ref[...], NORM_EPS)
        if with_ple:
            hn = _rms(h, ple_pre_g_ref[...], NORM_EPS).astype(BF16)
            gate = jax.nn.sigmoid(_dot(hn, w_gate_ref[...]))
            e = _dot(p_ref[...].astype(BF16), w_proj_ref[...]) * gate
            h = h + _rms(e, ple_post_g_ref[...], NORM_EPS)
        out_ref[...] = h


def _ffn(h, pre_g, w_gu, w_down, post_g, ple=None):
    n, d = h.shape
    d_ff = w_down.shape[0]
    tm = _pick_tile(n, 1024)
    tf = 256 if d_ff % 256 == 0 else LANES
    nj = d_ff // tf
    row = lambda i, j: (i, 0)
    const = lambda i, j: (0, 0)
    in_specs = [
        pl.BlockSpec((tm, d), row),
        pl.BlockSpec((1, d), const),
        pl.BlockSpec((d, tf), lambda i, j: (0, j)),
        pl.BlockSpec((d, tf), lambda i, j: (0, j + nj)),
        pl.BlockSpec((tf, d), lambda i, j: (j, 0)),
        pl.BlockSpec((1, d), const),
    ]
    args = [h, pre_g, w_gu, w_gu, w_down, post_g]
    if ple is not None:
        p, ple_pre_g, w_gate, w_proj, ple_post_g = ple
        in_specs += [
            pl.BlockSpec((tm, p.shape[1]), row),
            pl.BlockSpec((1, d), const),
            pl.BlockSpec((d, d), const),
            pl.BlockSpec((p.shape[1], d), const),
            pl.BlockSpec((1, d), const),
        ]
        args += [p, ple_pre_g, w_gate, w_proj, ple_post_g]
    return pl.pallas_call(
        functools.partial(_ffn_kernel, with_ple=ple is not None),
        grid=(n // tm, nj),
        in_specs=in_specs,
        out_specs=pl.BlockSpec((tm, d), row),
        out_shape=jax.ShapeDtypeStruct((n, d), F32),
        scratch_shapes=[pltpu.VMEM((tm, d), BF16), pltpu.VMEM((tm, d), F32)],
        compiler_params=_params("parallel", "arbitrary"),
        name="ffn_ple" if ple is not None else "ffn",
    )(*args)


def _rope(y, cos, sin_signed):
    w = y.shape[1]
    reps = w // LANES
    cos = jnp.concatenate([cos] * reps, axis=1)
    sin_signed = jnp.concatenate([sin_signed] * reps, axis=1)
    lane = lax.broadcasted_iota(jnp.int32, y.shape, 1)
    first_half = (lane % DF_QK_DIM) < (DF_QK_DIM // 2)
    half = DF_QK_DIM // 2
    partner = jnp.where(first_half, pltpu.roll(y, w - half, 1), pltpu.roll(y, half, 1))
    return y * cos + partner * sin_signed


def _proj_kernel(h_ref, g_ref, w_ref, cos_ref, sin_ref,
                 sbq_ref, sbk_ref, sbv_ref, dfq_ref, dfk_ref, dfv_ref, gate_ref):
    u = _rms(h_ref[...], g_ref[...], NORM_EPS).astype(BF16)
    cw = SB_W
    sb_scale = 1.0 / math.sqrt(SB_HEAD_DIM)
    df_scale = 1.0 / math.sqrt(DF_QK_DIM)

    def cols(c):
        return _dot(u, w_ref[:, c * cw:(c + 1) * cw])

    sbq_ref[...] = (cols(0) * sb_scale).astype(BF16)
    sbk_ref[...] = cols(1).astype(BF16)
    sbv_ref[...] = cols(2).astype(BF16)
    cos = cos_ref[...]
    sin = sin_ref[...]
    dfq_ref[...] = (_rope(cols(3), cos, sin) * df_scale).astype(BF16)
    dfk_ref[...] = _rope(cols(4), cos, sin).astype(BF16)
    dfv_ref[...] = cols(5).astype(BF16)
    n_gate = gate_ref.shape[1] // cw
    for c in range(n_gate):
        gate_ref[:, c * cw:(c + 1) * cw] = jax.nn.sigmoid(cols(6 + c)).astype(BF16)


def _proj(h, g, w_in, cos, sin):
    n, d = h.shape
    d_in = w_in.shape[1]
    tm = _pick_tile(n, 512)
    row = lambda i: (i, 0)
    const = lambda i: (0, 0)
    n_gate_cols = d_in - 6 * SB_W
    head_out = jax.ShapeDtypeStruct((n, SB_W), BF16)
    return pl.pallas_call(
        _proj_kernel,
        grid=(n // tm,),
        in_specs=[
            pl.BlockSpec((tm, d), row),
            pl.BlockSpec((1, d), const),
            pl.BlockSpec((d, d_in), const),
            pl.BlockSpec((tm, LANES), row),
            pl.BlockSpec((tm, LANES), row),
        ],
        out_specs=[pl.BlockSpec((tm, SB_W), row)] * 6 + [pl.BlockSpec((tm, n_gate_cols), row)],
        out_shape=[head_out] * 6 + [jax.ShapeDtypeStruct((n, n_gate_cols), BF16)],
        compiler_params=_params("parallel"),
        name="proj_in",
    )(h, g, w_in, cos, sin)


def _stack_halves(q):
    lane = lax.broadcasted_iota(jnp.int32, q.shape, 1)
    lo = lane < (LANES // 2)
    zero = jnp.zeros_like(q)
    return jnp.concatenate([jnp.where(lo, q, zero), jnp.where(lo, zero, q)], axis=0)


def _sb_kernel(q_ref, k_ref, v_ref, o_ref, acc_ref, c_ref, *, blk):
    qi = pl.program_id(2)
    q2 = _stack_halves(q_ref[...])
    r = lax.broadcasted_iota(jnp.int32, (blk, blk), 0)
    c = lax.broadcasted_iota(jnp.int32, (blk, blk), 1)
    tri = (r > c).astype(BF16)
    strict = jnp.concatenate([r > c, r > c], axis=0)

    def key_block(j, masked):
        start = pl.multiple_of(j * blk, blk)
        k = k_ref[pl.ds(start, blk), :]
        v = v_ref[pl.ds(start, blk), :]
        z = _dot_nt(q2, k)
        sp = jnp.log(1.0 + jnp.exp(-jnp.abs(z)))
        mz = jnp.minimum(z, 0.0)
        log_beta = mz - sp
        log_1m = (mz - z) - sp
        if masked:
            log_1m = jnp.where(strict, log_1m, 0.0)
        hi = log_1m.astype(BF16)
        lo = (log_1m - hi.astype(F32)).astype(BF16)
        later = _dot(hi, tri) + _dot(lo, tri)
        carry = c_ref[...]
        w = jnp.exp(log_beta + later + carry)
        if masked:
            w = jnp.where(strict, w, 0.0)
        acc_ref[...] += _dot(w.astype(BF16), v)
        c_ref[...] = carry + later[:, :1] + log_1m[:, :1]

    acc_ref[...] = jnp.zeros_like(acc_ref)
    c_ref[...] = jnp.zeros_like(c_ref)
    key_block(qi, True)

    def body(t, _):
        key_block(qi - 1 - t, False)
        return 0

    lax.fori_loop(0, qi, body, 0)
    acc = acc_ref[...]
    lane = lax.broadcasted_iota(jnp.int32, (blk, LANES), 1)
    o_ref[...] = jnp.where(lane < LANES // 2, acc[:blk], acc[blk:]).astype(o_ref.dtype)


def _diff_kernel(lp_ref, g_ref, q_ref, k_ref, v_ref, o_ref, acc_ref, m_ref, l_ref, *, blk, lam_init):
    qi = pl.program_id(2)
    q2 = _stack_halves(q_ref[...])
    r = lax.broadcasted_iota(jnp.int32, (2 * blk, blk), 0) % blk
    c = lax.broadcasted_iota(jnp.int32, (2 * blk, blk), 1)
    causal = c <= r

    def scores(j):
        start = pl.multiple_of(j * blk, blk)
        return _dot_nt(q2, k_ref[pl.ds(start, blk), :]), v_ref[pl.ds(start, blk), :]

    z, v = scores(qi)
    z = jnp.where(causal, z, -jnp.inf)
    m = jnp.max(z, axis=-1, keepdims=True)
    p = jnp.exp(z - m)
    m_ref[...] = m
    l_ref[...] = jnp.sum(p, axis=-1, keepdims=True)
    acc_ref[...] = _dot(p.astype(BF16), v)

    def body(t, _):
        z, v = scores(qi - 1 - t)
        m_prev = m_ref[...]
        m_new = jnp.maximum(m_prev, jnp.max(z, axis=-1, keepdims=True))
        alpha = jnp.exp(m_prev - m_new)
        p = jnp.exp(z - m_new)
        m_ref[...] = m_new
        l_ref[...] = alpha * l_ref[...] + jnp.sum(p, axis=-1, keepdims=True)
        acc_ref[...] = alpha * acc_ref[...] + _dot(p.astype(BF16), v)
        return 0

    lax.fori_loop(0, qi, body, 0)
    o = acc_ref[...] / l_ref[...]
    lp = lp_ref[...]
    lam = (jnp.exp(jnp.sum(lp[0:1] * lp[1:2], axis=-1, keepdims=True))
           - jnp.exp(jnp.sum(lp[2:3] * lp[3:4], axis=-1, keepdims=True)) + lam_init)
    o = o[:blk] - lam * o[blk:]
    o_ref[...] = (_rms(o, g_ref[...], SUBLN_EPS) * (1.0 - lam_init)).astype(o_ref.dtype)


def _attn_specs(s, blk):
    q_spec = pl.BlockSpec((blk, LANES), lambda b, h, i: (b * (s // blk) + i, h))
    kv_spec = pl.BlockSpec((s, LANES), lambda b, h, i: (b, h))
    return q_spec, kv_spec


def _sb_attention(q, k, v, batch, s):
    n, w = q.shape
    blk = _pick_tile(s, 256)
    q_spec, kv_spec = _attn_specs(s, blk)
    return pl.pallas_call(
        functools.partial(_sb_kernel, blk=blk),
        grid=(batch, w // LANES, s // blk),
        in_specs=[q_spec, kv_spec, kv_spec],
        out_specs=q_spec,
        out_shape=jax.ShapeDtypeStruct((n, w), BF16),
        scratch_shapes=[pltpu.VMEM((2 * blk, LANES), F32), pltpu.VMEM((2 * blk, 1), F32)],
        compiler_params=_params("parallel", "parallel", "arbitrary"),
        name="sb_attn",
    )(q, k, v)


def _diff_attention(lp, g, q, k, v, batch, s, lam_init):
    n, w = q.shape
    blk = _pick_tile(s, 256)
    q_spec, kv_spec = _attn_specs(s, blk)
    const = lambda b, h, i: (0, 0)
    return pl.pallas_call(
        functools.partial(_diff_kernel, blk=blk, lam_init=lam_init),
        grid=(batch, w // LANES, s // blk),
        in_specs=[pl.BlockSpec(lp.shape, const), pl.BlockSpec(g.shape, const), q_spec, kv_spec, kv_spec],
        out_specs=q_spec,
        out_shape=jax.ShapeDtypeStruct((n, w), BF16),
        scratch_shapes=[pltpu.VMEM((2 * blk, LANES), F32), pltpu.VMEM((2 * blk, 1), F32),
                        pltpu.VMEM((2 * blk, 1), F32)],
        compiler_params=_params("parallel", "parallel", "arbitrary"),
        name="diff_attn",
    )(lp, g, q, k, v)


def _merge_kernel(h_ref, osb_ref, odf_ref, gate_ref, wsb_ref, wdf_ref, wout_ref, g_ref, out_ref):
    d = h_ref.shape[1]
    y_sb = _dot(osb_ref[...], wsb_ref[...])
    y_df = _dot(odf_ref[...], wdf_ref[...])
    merged = gate_ref[:, :d].astype(F32) * y_sb + gate_ref[:, d:].astype(F32) * y_df
    out = _dot(merged.astype(BF16), wout_ref[...])
    out_ref[...] = h_ref[...] + _rms(out, g_ref[...], NORM_EPS)


def _merge(h, o_sb, o_df, gates, w_sb, w_df, w_out, g):
    n, d = h.shape
    tm = _pick_tile(n, 512)
    row = lambda i: (i, 0)
    const = lambda i: (0, 0)
    return pl.pallas_call(
        _merge_kernel,
        grid=(n // tm,),
        in_specs=[
            pl.BlockSpec((tm, d), row),
            pl.BlockSpec((tm, o_sb.shape[1]), row),
            pl.BlockSpec((tm, o_df.shape[1]), row),
            pl.BlockSpec((tm, gates.shape[1]), row),
            pl.BlockSpec(w_sb.shape, const),
            pl.BlockSpec(w_df.shape, const),
            pl.BlockSpec(w_out.shape, const),
            pl.BlockSpec((1, d), const),
        ],
        out_specs=pl.BlockSpec((tm, d), row),
        out_shape=jax.ShapeDtypeStruct((n, d), F32),
        compiler_params=_params("parallel"),
        name="merge_out",
    )(h, o_sb, o_df, gates, w_sb, w_df, w_out, g)


def _rope_tables(positions):
    half = DF_QK_DIM // 2
    inv_freq = ROPE_THETA ** (-jnp.arange(half, dtype=F32) / half)
    ang = positions.astype(F32).reshape(-1, 1) * inv_freq
    cos, sin = jnp.cos(ang), jnp.sin(ang)
    reps = LANES // DF_QK_DIM
    return (jnp.tile(jnp.concatenate([cos, cos], axis=1), (1, reps)),
            jnp.tile(jnp.concatenate([-sin, sin], axis=1), (1, reps)))


def kernel(x, p, positions, ffn1_pre_g, ffn1_w_gu, ffn1_w_down, ffn1_post_g, mix_pre_g, w_in, diff_lambda, diff_subln_g, w_branch_sb, w_branch_diff, w_out, mix_post_g, ffn2_pre_g, ffn2_w_gu, ffn2_w_down, ffn2_post_g, ple_pre_g, w_ple_gate, w_ple_proj, ple_post_g):
    b, s, d = x.shape
    depth = p.shape[0]
    n = b * s
    h = x.reshape(n, d)
    cos, sin = _rope_tables(positions)
    gain = lambda g, i: g[i].reshape(1, -1).astype(F32)
    wt = lambda w, i: w[i].astype(BF16)
    for i in range(depth):
        h = _ffn(h, gain(ffn1_pre_g, i), wt(ffn1_w_gu, i), wt(ffn1_w_down, i), gain(ffn1_post_g, i))
        sbq, sbk, sbv, dfq, dfk, dfv, gates = _proj(h, gain(mix_pre_g, i), wt(w_in, i), cos, sin)
        o_sb = _sb_attention(sbq, sbk, sbv, b, s)
        lam_init = 0.8 - 0.6 * math.exp(-0.3 * i)
        o_df = _diff_attention(diff_lambda[i].astype(F32), gain(diff_subln_g, i), dfq, dfk, dfv, b, s, lam_init)
        h = _merge(h, o_sb, o_df, gates, wt(w_branch_sb, i), wt(w_branch_diff, i), wt(w_out, i),
                   gain(mix_post_g, i))
        ple = (p[i].reshape(n, -1), gain(ple_pre_g, i), wt(w_ple_gate, i), wt(w_ple_proj, i),
               gain(ple_post_g, i))
        h = _ffn(h, gain(ffn2_pre_g, i), wt(ffn2_w_gu, i), wt(ffn2_w_down, i), gain(ffn2_post_g, i), ple=ple)
    return h.reshape(b, s, d)
```

```python
import functools
import math

import jax
import jax.numpy as jnp
from jax import lax
from jax.experimental import pallas as pl
from jax.experimental.pallas import tpu as pltpu

NORM_EPS = 1e-6
SUBLN_EPS = 1e-5
ROPE_THETA = 10000.0
SB_HEADS = 8
SB_HEAD_DIM = 64
DF_HEADS = 4
DF_QK_DIM = 64
DF_V_DIM = 2 * DF_QK_DIM
SB_W = SB_HEADS * SB_HEAD_DIM
DF_W = DF_HEADS * DF_V_DIM

LANES = 128
LOG2E = math.log2(math.e)
ATTN_BLOCK = 256
SB_GROUPS_PER_STEP = 4
DF_GROUPS_PER_STEP = 4
VMEM_LIMIT_BYTES = 48 * 1024 * 1024

F32 = jnp.float32
BF16 = jnp.bfloat16


def _pick_tile(n, target):
    t = min(n, target)
    while n % t:
        t //= 2
    return t


def _rms(x, g, eps):
    return x * lax.rsqrt(jnp.mean(x * x, axis=-1, keepdims=True) + eps) * g


def _dot(a, b):
    return jnp.dot(a, b, preferred_element_type=F32)


def _dot_nt(a, b):
    return lax.dot_general(a, b, (((1,), (1,)), ((), ())), preferred_element_type=F32)


def _params(*sem):
    return pltpu.CompilerParams(dimension_semantics=sem, vmem_limit_bytes=VMEM_LIMIT_BYTES)


def _ffn_kernel(*refs, with_ple):
    if with_ple:
        (h_ref, pre_g_ref, wg_ref, wu_ref, wd_ref, post_g_ref,
         p_ref, ple_pre_g_ref, w_gate_ref, w_proj_ref, ple_post_g_ref,
         out_ref, xn_ref, acc_ref) = refs
    else:
        h_ref, pre_g_ref, wg_ref, wu_ref, wd_ref, post_g_ref, out_ref, xn_ref, acc_ref = refs
    j = pl.program_id(1)

    @pl.when(j == 0)
    def _():
        xn_ref[...] = _rms(h_ref[...], pre_g_ref[...], NORM_EPS).astype(BF16)
        acc_ref[...] = jnp.zeros_like(acc_ref)

    xn = xn_ref[...]
    g = _dot(xn, wg_ref[...])
    u = _dot(xn, wu_ref[...])
    a = (g * jax.nn.sigmoid(g) * u).astype(BF16)
    acc_ref[...] += _dot(a, wd_ref[...])

    @pl.when(j == pl.num_programs(1) - 1)
    def _():
        h = h_ref[...] + 0.5 * _rms(acc_ref[...], post_g_ref[...], NORM_EPS)
        if with_ple:
            hn = _rms(h, ple_pre_g_ref[...], NORM_EPS).astype(BF16)
            gate = jax.nn.sigmoid(_dot(hn, w_gate_ref[...]))
            e = _dot(p_ref[...].astype(BF16), w_proj_ref[...]) * gate
            h = h + _rms(e, ple_post_g_ref[...], NORM_EPS)
        out_ref[...] = h


def _ffn(h, pre_g, w_gu, w_down, post_g, ple=None):
    n, d = h.shape
    d_ff = w_down.shape[0]
    tm = _pick_tile(n, 1024)
    tf = 256 if d_ff % 256 == 0 else LANES
    nj = d_ff // tf
    row = lambda i, j: (i, 0)
    const = lambda i, j: (0, 0)
    in_specs = [
        pl.BlockSpec((tm, d), row),
        pl.BlockSpec((1, d), const),
        pl.BlockSpec((d, tf), lambda i, j: (0, j)),
        pl.BlockSpec((d, tf), lambda i, j: (0, j + nj)),
        pl.BlockSpec((tf, d), lambda i, j: (j, 0)),
        pl.BlockSpec((1, d), const),
    ]
    args = [h, pre_g, w_gu, w_gu, w_down, post_g]
    if ple is not None:
        p, ple_pre_g, w_gate, w_proj, ple_post_g = ple
        in_specs += [
            pl.BlockSpec((tm, p.shape[1]), row),
            pl.BlockSpec((1, d), const),
            pl.BlockSpec((d, d), const),
            pl.BlockSpec((p.shape[1], d), const),
            pl.BlockSpec((1, d), const),
        ]
        args += [p, ple_pre_g, w_gate, w_proj, ple_post_g]
    return pl.pallas_call(
        functools.partial(_ffn_kernel, with_ple=ple is not None),
        grid=(n // tm, nj),
        in_specs=in_specs,
        out_specs=pl.BlockSpec((tm, d), row),
        out_shape=jax.ShapeDtypeStruct((n, d), F32),
        scratch_shapes=[pltpu.VMEM((tm, d), BF16), pltpu.VMEM((tm, d), F32)],
        compiler_params=_params("parallel", "arbitrary"),
        name="ffn_ple" if ple is not None else "ffn",
    )(*args)


def _rope(y, cos, sin_signed):
    w = y.shape[1]
    reps = w // LANES
    cos = jnp.concatenate([cos] * reps, axis=1)
    sin_signed = jnp.concatenate([sin_signed] * reps, axis=1)
    lane = lax.broadcasted_iota(jnp.int32, y.shape, 1)
    first_half = (lane % DF_QK_DIM) < (DF_QK_DIM // 2)
    half = DF_QK_DIM // 2
    partner = jnp.where(first_half, pltpu.roll(y, w - half, 1), pltpu.roll(y, half, 1))
    return y * cos + partner * sin_signed


def _proj_kernel(h_ref, g_ref, w_ref, cos_ref, sin_ref,
                 sbq_ref, sbk_ref, sbv_ref, dfq_ref, dfk_ref, dfv_ref, gate_ref):
    u = _rms(h_ref[...], g_ref[...], NORM_EPS).astype(BF16)
    cw = SB_W
    sb_scale = LOG2E / math.sqrt(SB_HEAD_DIM)
    df_scale = LOG2E / math.sqrt(DF_QK_DIM)

    def cols(c):
        return _dot(u, w_ref[:, c * cw:(c + 1) * cw])

    sbq_ref[...] = (cols(0) * sb_scale).astype(BF16)
    sbk_ref[...] = cols(1).astype(BF16)
    sbv_ref[...] = cols(2).astype(BF16)
    cos = cos_ref[...]
    sin = sin_ref[...]
    dfq_ref[...] = (_rope(cols(3), cos, sin) * df_scale).astype(BF16)
    dfk_ref[...] = _rope(cols(4), cos, sin).astype(BF16)
    dfv_ref[...] = cols(5).astype(BF16)
    n_gate = gate_ref.shape[1] // cw
    for c in range(n_gate):
        gate_ref[:, c * cw:(c + 1) * cw] = jax.nn.sigmoid(cols(6 + c)).astype(BF16)


def _proj(h, g, w_in, cos, sin):
    n, d = h.shape
    d_in = w_in.shape[1]
    tm = _pick_tile(n, 512)
    row = lambda i: (i, 0)
    const = lambda i: (0, 0)
    n_gate_cols = d_in - 6 * SB_W
    head_out = jax.ShapeDtypeStruct((n, SB_W), BF16)
    return pl.pallas_call(
        _proj_kernel,
        grid=(n // tm,),
        in_specs=[
            pl.BlockSpec((tm, d), row),
            pl.BlockSpec((1, d), const),
            pl.BlockSpec((d, d_in), const),
            pl.BlockSpec((tm, LANES), row),
            pl.BlockSpec((tm, LANES), row),
        ],
        out_specs=[pl.BlockSpec((tm, SB_W), row)] * 6 + [pl.BlockSpec((tm, n_gate_cols), row)],
        out_shape=[head_out] * 6 + [jax.ShapeDtypeStruct((n, n_gate_cols), BF16)],
        compiler_params=_params("parallel"),
        name="proj_in",
    )(h, g, w_in, cos, sin)


def _stack_halves(q):
    lane = lax.broadcasted_iota(jnp.int32, q.shape, 1)
    lo = lane < (LANES // 2)
    zero = jnp.zeros_like(q)
    return jnp.concatenate([jnp.where(lo, q, zero), jnp.where(lo, zero, q)], axis=0)


def _lane_chunks(x):
    return [x[:, c * LANES:(c + 1) * LANES] for c in range(x.shape[1] // LANES)]


def _group(ref, g, rows=None):
    cols = slice(g * LANES, (g + 1) * LANES)
    return ref[:, cols] if rows is None else ref[rows, cols]


def _sb_kernel(q_ref, k_ref, v_ref, o_ref, acc_ref, c_ref, *, blk, groups):
    qi = pl.program_id(2)
    q2 = [_stack_halves(_group(q_ref, g)) for g in range(groups)]
    r = lax.broadcasted_iota(jnp.int32, (2 * blk, blk), 0) % blk
    c = lax.broadcasted_iota(jnp.int32, (2 * blk, blk), 1)
    strict = c < r
    tri2 = (r > c).astype(BF16)

    def key_block(j, masked):
        rows = pl.ds(pl.multiple_of(j * blk, blk), blk)
        gs = range(groups)
        zb = [_dot_nt(q2[g], _group(k_ref, g, rows)) for g in gs]
        log_beta, log_1m, split = [], [], []
        for g in gs:
            sp = jnp.log(1.0 + jnp.exp2(-jnp.abs(zb[g]))) * LOG2E
            lb = jnp.minimum(zb[g], 0.0) - sp
            l1m = lb - zb[g]
            if masked:
                l1m = jnp.where(strict, l1m, 0.0)
            hi = l1m.astype(BF16)
            lo = (l1m - hi.astype(F32)).astype(BF16)
            log_beta.append(lb)
            log_1m.append(l1m)
            split.append(jnp.concatenate([hi, lo], axis=1))
        later = [_dot(split[g], tri2) for g in gs]
        w = []
        for g in gs:
            wg = jnp.exp2(log_beta[g] + later[g])
            if masked:
                wg = jnp.where(strict, wg, 0.0)
            w.append(wg.astype(BF16))
        pv = [_dot(w[g], _group(v_ref, g, rows)) for g in gs]
        for g in gs:
            carry = c_ref[g]
            acc_ref[g] += jnp.exp2(carry) * pv[g]
            c_ref[g] = carry + (later[g][:, :1] + log_1m[g][:, :1])

    acc_ref[...] = jnp.zeros_like(acc_ref)
    c_ref[...] = jnp.zeros_like(c_ref)
    key_block(qi, True)

    def body(t, _):
        key_block(qi - 1 - t, False)
        return 0

    lax.fori_loop(0, qi, body, 0)
    lane = lax.broadcasted_iota(jnp.int32, (blk, LANES), 1)
    for g in range(groups):
        acc = acc_ref[g]
        o_ref[:, g * LANES:(g + 1) * LANES] = jnp.where(lane < LANES // 2, acc[:blk], acc[blk:]).astype(o_ref.dtype)


def _diff_kernel(lp_ref, g_ref, q_ref, k_ref, v_ref, o_ref, acc_ref, m_ref, l_ref, *, blk, groups, lam_init):
    qi = pl.program_id(2)
    q2 = [_stack_halves(_group(q_ref, g)) for g in range(groups)]
    r = lax.broadcasted_iota(jnp.int32, (2 * blk, blk), 0) % blk
    c = lax.broadcasted_iota(jnp.int32, (2 * blk, blk), 1)
    causal = c <= r

    def key_block(j, first):
        rows = pl.ds(pl.multiple_of(j * blk, blk), blk)
        gs = range(groups)
        zb = [_dot_nt(q2[g], _group(k_ref, g, rows)) for g in gs]
        m_prev, m_new, p, psum = [], [], [], []
        for g in gs:
            z = jnp.where(causal, zb[g], -jnp.inf) if first else zb[g]
            zc = _lane_chunks(z)
            zmax = jnp.max(functools.reduce(jnp.maximum, zc), axis=-1, keepdims=True)
            if first:
                m_prev.append(None)
                m_new.append(jnp.broadcast_to(zmax, (2 * blk, LANES)))
            else:
                m_prev.append(m_ref[g])
                m_new.append(jnp.maximum(m_prev[g], zmax))
            pc = [jnp.exp2(c - m_new[g]) for c in zc]
            psum.append(functools.reduce(jnp.add, pc))
            p.append(jnp.concatenate(pc, axis=1).astype(BF16))
        pv = [_dot(p[g], _group(v_ref, g, rows)) for g in gs]
        for g in gs:
            if first:
                l_ref[g] = psum[g]
                acc_ref[g] = pv[g]
            else:
                alpha = jnp.exp2(m_prev[g] - m_new[g])
                l_ref[g] = alpha * l_ref[g] + psum[g]
                acc_ref[g] = alpha * acc_ref[g] + pv[g]
            m_ref[g] = m_new[g]

    key_block(qi, True)

    def body(t, _):
        key_block(qi - 1 - t, False)
        return 0

    lax.fori_loop(0, qi, body, 0)
    lp = lp_ref[...]
    lam = (jnp.exp(jnp.sum(lp[0:1] * lp[1:2], axis=-1, keepdims=True))
           - jnp.exp(jnp.sum(lp[2:3] * lp[3:4], axis=-1, keepdims=True)) + lam_init)
    for g in range(groups):
        o = acc_ref[g] / jnp.sum(l_ref[g], axis=-1, keepdims=True)
        o = o[:blk] - lam * o[blk:]
        o_ref[:, g * LANES:(g + 1) * LANES] = (
            _rms(o, g_ref[...], SUBLN_EPS) * (1.0 - lam_init)).astype(o_ref.dtype)


def _attn_specs(s, blk, groups):
    w = groups * LANES
    q_spec = pl.BlockSpec((blk, w), lambda b, h, i: (b * (s // blk) + i, h))
    kv_spec = pl.BlockSpec((s, w), lambda b, h, i: (b, h))
    return q_spec, kv_spec


def _sb_attention(q, k, v, batch, s):
    n, w = q.shape
    blk = _pick_tile(s, ATTN_BLOCK)
    groups = SB_GROUPS_PER_STEP
    q_spec, kv_spec = _attn_specs(s, blk, groups)
    stat = pltpu.VMEM((groups, 2 * blk, LANES), F32)
    return pl.pallas_call(
        functools.partial(_sb_kernel, blk=blk, groups=groups),
        grid=(batch, w // (groups * LANES), s // blk),
        in_specs=[q_spec, kv_spec, kv_spec],
        out_specs=q_spec,
        out_shape=jax.ShapeDtypeStruct((n, w), BF16),
        scratch_shapes=[stat, stat],
        compiler_params=_params("parallel", "parallel", "arbitrary"),
        name="sb_attn",
    )(q, k, v)


def _diff_attention(lp, g, q, k, v, batch, s, lam_init):
    n, w = q.shape
    blk = _pick_tile(s, ATTN_BLOCK)
    groups = DF_GROUPS_PER_STEP
    q_spec, kv_spec = _attn_specs(s, blk, groups)
    const = lambda b, h, i: (0, 0)
    stat = pltpu.VMEM((groups, 2 * blk, LANES), F32)
    return pl.pallas_call(
        functools.partial(_diff_kernel, blk=blk, groups=groups, lam_init=lam_init),
        grid=(batch, w // (groups * LANES), s // blk),
        in_specs=[pl.BlockSpec(lp.shape, const), pl.BlockSpec(g.shape, const), q_spec, kv_spec, kv_spec],
        out_specs=q_spec,
        out_shape=jax.ShapeDtypeStruct((n, w), BF16),
        scratch_shapes=[stat, stat, stat],
        compiler_params=_params("parallel", "parallel", "arbitrary"),
        name="diff_attn",
    )(lp, g, q, k, v)


def _merge_kernel(h_ref, osb_ref, odf_ref, gate_ref, wsb_ref, wdf_ref, wout_ref, g_ref, out_ref):
    d = h_ref.shape[1]
    y_sb = _dot(osb_ref[...], wsb_ref[...])
    y_df = _dot(odf_ref[...], wdf_ref[...])
    merged = gate_ref[:, :d].astype(F32) * y_sb + gate_ref[:, d:].astype(F32) * y_df
    out = _dot(merged.astype(BF16), wout_ref[...])
    out_ref[...] = h_ref[...] + _rms(out, g_ref[...], NORM_EPS)


def _merge(h, o_sb, o_df, gates, w_sb, w_df, w_out, g):
    n, d = h.shape
    tm = _pick_tile(n, 512)
    row = lambda i: (i, 0)
    const = lambda i: (0, 0)
    return pl.pallas_call(
        _merge_kernel,
        grid=(n // tm,),
        in_specs=[
            pl.BlockSpec((tm, d), row),
            pl.BlockSpec((tm, o_sb.shape[1]), row),
            pl.BlockSpec((tm, o_df.shape[1]), row),
            pl.BlockSpec((tm, gates.shape[1]), row),
            pl.BlockSpec(w_sb.shape, const),
            pl.BlockSpec(w_df.shape, const),
            pl.BlockSpec(w_out.shape, const),
            pl.BlockSpec((1, d), const),
        ],
        out_specs=pl.BlockSpec((tm, d), row),
        out_shape=jax.ShapeDtypeStruct((n, d), F32),
        compiler_params=_params("parallel"),
        name="merge_out",
    )(h, o_sb, o_df, gates, w_sb, w_df, w_out, g)


def _rope_tables(positions):
    half = DF_QK_DIM // 2
    inv_freq = ROPE_THETA ** (-jnp.arange(half, dtype=F32) / half)
    ang = positions.astype(F32).reshape(-1, 1) * inv_freq
    cos, sin = jnp.cos(ang), jnp.sin(ang)
    reps = LANES // DF_QK_DIM
    return (jnp.tile(jnp.concatenate([cos, cos], axis=1), (1, reps)),
            jnp.tile(jnp.concatenate([-sin, sin], axis=1), (1, reps)))


def kernel(x, p, positions, ffn1_pre_g, ffn1_w_gu, ffn1_w_down, ffn1_post_g, mix_pre_g, w_in, diff_lambda, diff_subln_g, w_branch_sb, w_branch_diff, w_out, mix_post_g, ffn2_pre_g, ffn2_w_gu, ffn2_w_down, ffn2_post_g, ple_pre_g, w_ple_gate, w_ple_proj, ple_post_g):
    b, s, d = x.shape
    depth = p.shape[0]
    n = b * s
    h = x.reshape(n, d)
    cos, sin = _rope_tables(positions)
    gain = lambda g, i: g[i].reshape(1, -1).astype(F32)
    wt = lambda w, i: w[i].astype(BF16)
    for i in range(depth):
        h = _ffn(h, gain(ffn1_pre_g, i), wt(ffn1_w_gu, i), wt(ffn1_w_down, i), gain(ffn1_post_g, i))
        sbq, sbk, sbv, dfq, dfk, dfv, gates = _proj(h, gain(mix_pre_g, i), wt(w_in, i), cos, sin)
        o_sb = _sb_attention(sbq, sbk, sbv, b, s)
        lam_init = 0.8 - 0.6 * math.exp(-0.3 * i)
        o_df = _diff_attention(diff_lambda[i].astype(F32), gain(diff_subln_g, i), dfq, dfk, dfv, b, s, lam_init)
        h = _merge(h, o_sb, o_df, gates, wt(w_branch_sb, i), wt(w_branch_diff, i), wt(w_out, i),
                   gain(mix_post_g, i))
        ple = (p[i].reshape(n, -1), gain(ple_pre_g, i), wt(w_ple_gate, i), wt(w_ple_proj, i),
               gain(ple_post_g, i))
        h = _ffn(h, gain(ffn2_pre_g, i), wt(ffn2_w_gu, i), wt(ffn2_w_down, i), gain(ffn2_post_g, i), ple=ple)
    return h.reshape(b, s, d)
```

```python
import functools
import math

import jax
import jax.numpy as jnp
from jax import lax
from jax.experimental import pallas as pl
from jax.experimental.pallas import tpu as pltpu

NORM_EPS = 1e-6
SUBLN_EPS = 1e-5
ROPE_THETA = 10000.0
SB_HEADS = 8
SB_HEAD_DIM = 64
DF_HEADS = 4
DF_QK_DIM = 64
DF_V_DIM = 2 * DF_QK_DIM
SB_W = SB_HEADS * SB_HEAD_DIM
DF_W = DF_HEADS * DF_V_DIM

LANES = 128
LOG2E = math.log2(math.e)
SIGN_BIT = -2 ** 31
ATTN_BLOCK = 256
SB_GROUPS_PER_STEP = 4
DF_GROUPS_PER_STEP = 4
VMEM_LIMIT_BYTES = 48 * 1024 * 1024

F32 = jnp.float32
BF16 = jnp.bfloat16


def _pick_tile(n, target):
    t = min(n, target)
    while n % t:
        t //= 2
    return t


def _rms(x, g, eps):
    return x * lax.rsqrt(jnp.mean(x * x, axis=-1, keepdims=True) + eps) * g


def _dot(a, b):
    return jnp.dot(a, b, preferred_element_type=F32)


def _dot_nt(a, b):
    return lax.dot_general(a, b, (((1,), (1,)), ((), ())), preferred_element_type=F32)


def _params(*sem):
    return pltpu.CompilerParams(dimension_semantics=sem, vmem_limit_bytes=VMEM_LIMIT_BYTES)


def _ffn_kernel(*refs, with_ple):
    if with_ple:
        (h_ref, pre_g_ref, wg_ref, wu_ref, wd_ref, post_g_ref,
         p_ref, ple_pre_g_ref, w_gate_ref, w_proj_ref, ple_post_g_ref,
         out_ref, xn_ref, acc_ref) = refs
    else:
        h_ref, pre_g_ref, wg_ref, wu_ref, wd_ref, post_g_ref, out_ref, xn_ref, acc_ref = refs
    j = pl.program_id(1)

    @pl.when(j == 0)
    def _():
        xn_ref[...] = _rms(h_ref[...], pre_g_ref[...], NORM_EPS).astype(BF16)
        acc_ref[...] = jnp.zeros_like(acc_ref)

    xn = xn_ref[...]
    g = _dot(xn, wg_ref[...])
    u = _dot(xn, wu_ref[...])
    a = (g * jax.nn.sigmoid(g) * u).astype(BF16)
    acc_ref[...] += _dot(a, wd_ref[...])

    @pl.when(j == pl.num_programs(1) - 1)
    def _():
        h = h_ref[...] + 0.5 * _rms(acc_ref[...], post_g_ref[...], NORM_EPS)
        if with_ple:
            hn = _rms(h, ple_pre_g_ref[...], NORM_EPS).astype(BF16)
            gate = jax.nn.sigmoid(_dot(hn, w_gate_ref[...]))
            e = _dot(p_ref[...].astype(BF16), w_proj_ref[...]) * gate
            h = h + _rms(e, ple_post_g_ref[...], NORM_EPS)
        out_ref[...] = h


def _ffn(h, pre_g, w_gu, w_down, post_g, ple=None):
    n, d = h.shape
    d_ff = w_down.shape[0]
    tm = _pick_tile(n, 1024)
    tf = 256 if d_ff % 256 == 0 else LANES
    nj = d_ff // tf
    row = lambda i, j: (i, 0)
    const = lambda i, j: (0, 0)
    in_specs = [
        pl.BlockSpec((tm, d), row),
        pl.BlockSpec((1, d), const),
        pl.BlockSpec((d, tf), lambda i, j: (0, j)),
        pl.BlockSpec((d, tf), lambda i, j: (0, j + nj)),
        pl.BlockSpec((tf, d), lambda i, j: (j, 0)),
        pl.BlockSpec((1, d), const),
    ]
    args = [h, pre_g, w_gu, w_gu, w_down, post_g]
    if ple is not None:
        p, ple_pre_g, w_gate, w_proj, ple_post_g = ple
        in_specs += [
            pl.BlockSpec((tm, p.shape[1]), row),
            pl.BlockSpec((1, d), const),
            pl.BlockSpec((d, d), const),
            pl.BlockSpec((p.shape[1], d), const),
            pl.BlockSpec((1, d), const),
        ]
        args += [p, ple_pre_g, w_gate, w_proj, ple_post_g]
    return pl.pallas_call(
        functools.partial(_ffn_kernel, with_ple=ple is not None),
        grid=(n // tm, nj),
        in_specs=in_specs,
        out_specs=pl.BlockSpec((tm, d), row),
        out_shape=jax.ShapeDtypeStruct((n, d), F32),
        scratch_shapes=[pltpu.VMEM((tm, d), BF16), pltpu.VMEM((tm, d), F32)],
        compiler_params=_params("parallel", "arbitrary"),
        name="ffn_ple" if ple is not None else "ffn",
    )(*args)


def _rope(y, cos, sin_signed):
    w = y.shape[1]
    reps = w // LANES
    cos = jnp.concatenate([cos] * reps, axis=1)
    sin_signed = jnp.concatenate([sin_signed] * reps, axis=1)
    lane = lax.broadcasted_iota(jnp.int32, y.shape, 1)
    first_half = (lane % DF_QK_DIM) < (DF_QK_DIM // 2)
    half = DF_QK_DIM // 2
    partner = jnp.where(first_half, pltpu.roll(y, w - half, 1), pltpu.roll(y, half, 1))
    return y * cos + partner * sin_signed


def _proj_kernel(h_ref, g_ref, w_ref, cos_ref, sin_ref,
                 sbq_ref, sbk_ref, sbv_ref, dfq_ref, dfk_ref, dfv_ref, gate_ref):
    u = _rms(h_ref[...], g_ref[...], NORM_EPS).astype(BF16)
    cw = SB_W
    sb_scale = LOG2E / math.sqrt(SB_HEAD_DIM)
    df_scale = LOG2E / math.sqrt(DF_QK_DIM)

    def cols(c):
        return _dot(u, w_ref[:, c * cw:(c + 1) * cw])

    sbq_ref[...] = (cols(0) * sb_scale).astype(BF16)
    sbk_ref[...] = cols(1).astype(BF16)
    sbv_ref[...] = cols(2).astype(BF16)
    cos = cos_ref[...]
    sin = sin_ref[...]
    dfq_ref[...] = (_rope(cols(3), cos, sin) * df_scale).astype(BF16)
    dfk_ref[...] = _rope(cols(4), cos, sin).astype(BF16)
    dfv_ref[...] = cols(5).astype(BF16)
    n_gate = gate_ref.shape[1] // cw
    for c in range(n_gate):
        gate_ref[:, c * cw:(c + 1) * cw] = jax.nn.sigmoid(cols(6 + c)).astype(BF16)


def _proj(h, g, w_in, cos, sin):
    n, d = h.shape
    d_in = w_in.shape[1]
    tm = _pick_tile(n, 512)
    row = lambda i: (i, 0)
    const = lambda i: (0, 0)
    n_gate_cols = d_in - 6 * SB_W
    head_out = jax.ShapeDtypeStruct((n, SB_W), BF16)
    return pl.pallas_call(
        _proj_kernel,
        grid=(n // tm,),
        in_specs=[
            pl.BlockSpec((tm, d), row),
            pl.BlockSpec((1, d), const),
            pl.BlockSpec((d, d_in), const),
            pl.BlockSpec((tm, LANES), row),
            pl.BlockSpec((tm, LANES), row),
        ],
        out_specs=[pl.BlockSpec((tm, SB_W), row)] * 6 + [pl.BlockSpec((tm, n_gate_cols), row)],
        out_shape=[head_out] * 6 + [jax.ShapeDtypeStruct((n, n_gate_cols), BF16)],
        compiler_params=_params("parallel"),
        name="proj_in",
    )(h, g, w_in, cos, sin)


def _stack_halves(q):
    lane = lax.broadcasted_iota(jnp.int32, q.shape, 1)
    lo = lane < (LANES // 2)
    zero = jnp.zeros_like(q)
    return jnp.concatenate([jnp.where(lo, q, zero), jnp.where(lo, zero, q)], axis=0)


def _lane_chunks(x):
    return [x[:, c * LANES:(c + 1) * LANES] for c in range(x.shape[1] // LANES)]


def _group(ref, g, rows=None):
    cols = slice(g * LANES, (g + 1) * LANES)
    return ref[:, cols] if rows is None else ref[rows, cols]


def _sb_kernel(q_ref, k_ref, v_ref, o_ref, acc_ref, c_ref, s_ref, w_ref, z_ref, *, blk, groups):
    qi = pl.program_id(2)
    gs = range(groups)
    q2 = [_stack_halves(_group(q_ref, g)) for g in gs]
    r = lax.broadcasted_iota(jnp.int32, (2 * blk, blk), 0) % blk
    c = lax.broadcasted_iota(jnp.int32, (2 * blk, blk), 1)
    strict = c < r
    tri2 = (r >= c).astype(BF16)

    def rows_of(j):
        return pl.ds(pl.multiple_of(j * blk, blk), blk)

    def score(g, j):
        return _dot_nt(q2[g], _group(k_ref, g, rows_of(j)))

    def suffix_sum(z, masked):
        neg_abs = lax.bitcast_convert_type(lax.bitcast_convert_type(z, jnp.int32) | SIGN_BIT, F32)
        sp = jnp.log(1.0 + jnp.exp2(neg_abs)) * LOG2E
        n = jnp.maximum(z, 0.0) + sp
        if masked:
            n = jnp.where(strict, n, 0.0)
        hi = n.astype(BF16)
        lo = (n - hi.astype(F32)).astype(BF16)
        return _dot(jnp.concatenate([hi, lo], axis=1), tri2)

    def store_weights(slot, g, z, suffix, masked):
        wg = jnp.exp2(z - suffix)
        if masked:
            wg = jnp.where(strict, wg, 0.0)
        w_ref[slot, g] = wg.astype(BF16)
        carry = c_ref[g]
        s_ref[slot, g] = jnp.exp2(-carry)
        c_ref[g] = carry + suffix[:, :1]

    def accumulate(slot, g, j):
        acc_ref[g] += s_ref[slot, g] * _dot(w_ref[slot, g], _group(v_ref, g, rows_of(j)))

    def step(cur, j):
        z, suffix = [], []
        for g in gs:
            z.append(z_ref[cur, g])
            suffix.append(suffix_sum(z[g], False))
            z_ref[1 - cur, g] = score(g, jnp.maximum(j - 1, 0))
            accumulate(cur, g, j + 1)
            if g > 0:
                store_weights(1 - cur, g - 1, z[g - 1], suffix[g - 1], False)
        store_weights(1 - cur, groups - 1, z[-1], suffix[-1], False)

    acc_ref[...] = jnp.zeros_like(acc_ref)
    c_ref[...] = jnp.zeros_like(c_ref)
    z = [score(g, qi) for g in gs]
    suffix = [suffix_sum(z[g], True) for g in gs]
    for g in gs:
        store_weights(0, g, z[g], suffix[g], True)
        z_ref[0, g] = score(g, jnp.maximum(qi - 1, 0))

    def pair(t, _):
        j = qi - 1 - 2 * t
        step(0, j)
        step(1, j - 1)
        return 0

    lax.fori_loop(0, qi // 2, pair, 0)

    @pl.when(qi % 2 == 1)
    def _():
        step(0, 0)
        for g in gs:
            accumulate(1, g, 0)

    @pl.when(qi % 2 == 0)
    def _():
        for g in gs:
            accumulate(0, g, 0)

    lane = lax.broadcasted_iota(jnp.int32, (blk, LANES), 1)
    for g in range(groups):
        acc = acc_ref[g]
        o_ref[:, g * LANES:(g + 1) * LANES] = jnp.where(lane < LANES // 2, acc[:blk], acc[blk:]).astype(o_ref.dtype)


def _diff_kernel(lp_ref, g_ref, q_ref, k_ref, v_ref, o_ref, acc_ref, m_ref, l_ref, a_ref, p_ref, z_ref,
                 *, blk, groups, lam_init):
    qi = pl.program_id(2)
    gs = range(groups)
    q2 = [_stack_halves(_group(q_ref, g)) for g in gs]
    r = lax.broadcasted_iota(jnp.int32, (2 * blk, blk), 0) % blk
    c = lax.broadcasted_iota(jnp.int32, (2 * blk, blk), 1)
    causal = c <= r

    def rows_of(j):
        return pl.ds(pl.multiple_of(j * blk, blk), blk)

    def score(g, j):
        return _dot_nt(q2[g], _group(k_ref, g, rows_of(j)))

    def store_probs(slot, g, z, first):
        if first:
            z = jnp.where(causal, z, -jnp.inf)
        zc = _lane_chunks(z)
        zmax = jnp.max(functools.reduce(jnp.maximum, zc), axis=-1, keepdims=True)
        if first:
            m_new = jnp.broadcast_to(zmax, (2 * blk, LANES))
        else:
            m_prev = m_ref[g]
            m_new = jnp.maximum(m_prev, zmax)
        pc = [jnp.exp2(x - m_new) for x in zc]
        psum = functools.reduce(jnp.add, pc)
        p_ref[slot, g] = jnp.concatenate(pc, axis=1).astype(BF16)
        if first:
            l_ref[g] = psum
            a_ref[slot, g] = jnp.ones((2 * blk, LANES), F32)
        else:
            alpha = jnp.exp2(m_prev - m_new)
            l_ref[g] = alpha * l_ref[g] + psum
            a_ref[slot, g] = alpha
        m_ref[g] = m_new

    def accumulate(slot, g, j):
        acc_ref[g] = a_ref[slot, g] * acc_ref[g] + _dot(p_ref[slot, g], _group(v_ref, g, rows_of(j)))

    def step(cur, j):
        for g in gs:
            store_probs(1 - cur, g, z_ref[cur, g], False)
            z_ref[1 - cur, g] = score(g, jnp.maximum(j - 1, 0))
            accumulate(cur, g, j + 1)

    acc_ref[...] = jnp.zeros_like(acc_ref)
    for g in gs:
        store_probs(0, g, score(g, qi), True)
        z_ref[0, g] = score(g, jnp.maximum(qi - 1, 0))

    def pair(t, _):
        j = qi - 1 - 2 * t
        step(0, j)
        step(1, j - 1)
        return 0

    lax.fori_loop(0, qi // 2, pair, 0)

    @pl.when(qi % 2 == 1)
    def _():
        step(0, 0)
        for g in gs:
            accumulate(1, g, 0)

    @pl.when(qi % 2 == 0)
    def _():
        for g in gs:
            accumulate(0, g, 0)

    lp = lp_ref[...]
    lam = (jnp.exp(jnp.sum(lp[0:1] * lp[1:2], axis=-1, keepdims=True))
           - jnp.exp(jnp.sum(lp[2:3] * lp[3:4], axis=-1, keepdims=True)) + lam_init)
    for g in range(groups):
        o = acc_ref[g] / jnp.sum(l_ref[g], axis=-1, keepdims=True)
        o = o[:blk] - lam * o[blk:]
        o_ref[:, g * LANES:(g + 1) * LANES] = (
            _rms(o, g_ref[...], SUBLN_EPS) * (1.0 - lam_init)).astype(o_ref.dtype)


def _attn_specs(s, blk, groups):
    w = groups * LANES
    q_spec = pl.BlockSpec((blk, w), lambda b, h, i: (b * (s // blk) + i, h))
    kv_spec = pl.BlockSpec((s, w), lambda b, h, i: (b, h))
    return q_spec, kv_spec


def _sb_attention(q, k, v, batch, s):
    n, w = q.shape
    blk = _pick_tile(s, ATTN_BLOCK)
    groups = SB_GROUPS_PER_STEP
    q_spec, kv_spec = _attn_specs(s, blk, groups)
    stat = pltpu.VMEM((groups, 2 * blk, LANES), F32)
    tile = (2, groups, 2 * blk, blk)
    return pl.pallas_call(
        functools.partial(_sb_kernel, blk=blk, groups=groups),
        grid=(batch, w // (groups * LANES), s // blk),
        in_specs=[q_spec, kv_spec, kv_spec],
        out_specs=q_spec,
        out_shape=jax.ShapeDtypeStruct((n, w), BF16),
        scratch_shapes=[stat, stat, pltpu.VMEM((2, groups, 2 * blk, LANES), F32),
                        pltpu.VMEM(tile, BF16), pltpu.VMEM(tile, F32)],
        compiler_params=_params("parallel", "parallel", "arbitrary"),
        name="sb_attn",
    )(q, k, v)


def _diff_attention(lp, g, q, k, v, batch, s, lam_init):
    n, w = q.shape
    blk = _pick_tile(s, ATTN_BLOCK)
    groups = DF_GROUPS_PER_STEP
    q_spec, kv_spec = _attn_specs(s, blk, groups)
    const = lambda b, h, i: (0, 0)
    stat = pltpu.VMEM((groups, 2 * blk, LANES), F32)
    tile = (2, groups, 2 * blk, blk)
    return pl.pallas_call(
        functools.partial(_diff_kernel, blk=blk, groups=groups, lam_init=lam_init),
        grid=(batch, w // (groups * LANES), s // blk),
        in_specs=[pl.BlockSpec(lp.shape, const), pl.BlockSpec(g.shape, const), q_spec, kv_spec, kv_spec],
        out_specs=q_spec,
        out_shape=jax.ShapeDtypeStruct((n, w), BF16),
        scratch_shapes=[stat, stat, stat, pltpu.VMEM((2, groups, 2 * blk, LANES), F32),
                        pltpu.VMEM(tile, BF16), pltpu.VMEM(tile, F32)],
        compiler_params=_params("parallel", "parallel", "arbitrary"),
        name="diff_attn",
    )(lp, g, q, k, v)


def _merge_kernel(h_ref, osb_ref, odf_ref, gate_ref, wsb_ref, wdf_ref, wout_ref, g_ref, out_ref):
    d = h_ref.shape[1]
    y_sb = _dot(osb_ref[...], wsb_ref[...])
    y_df = _dot(odf_ref[...], wdf_ref[...])
    merged = gate_ref[:, :d].astype(F32) * y_sb + gate_ref[:, d:].astype(F32) * y_df
    out = _dot(merged.astype(BF16), wout_ref[...])
    out_ref[...] = h_ref[...] + _rms(out, g_ref[...], NORM_EPS)


def _merge(h, o_sb, o_df, gates, w_sb, w_df, w_out, g):
    n, d = h.shape
    tm = _pick_tile(n, 512)
    row = lambda i: (i, 0)
    const = lambda i: (0, 0)
    return pl.pallas_call(
        _merge_kernel,
        grid=(n // tm,),
        in_specs=[
            pl.BlockSpec((tm, d), row),
            pl.BlockSpec((tm, o_sb.shape[1]), row),
            pl.BlockSpec((tm, o_df.shape[1]), row),
            pl.BlockSpec((tm, gates.shape[1]), row),
            pl.BlockSpec(w_sb.shape, const),
            pl.BlockSpec(w_df.shape, const),
            pl.BlockSpec(w_out.shape, const),
            pl.BlockSpec((1, d), const),
        ],
        out_specs=pl.BlockSpec((tm, d), row),
        out_shape=jax.ShapeDtypeStruct((n, d), F32),
        compiler_params=_params("parallel"),
        name="merge_out",
    )(h, o_sb, o_df, gates, w_sb, w_df, w_out, g)


def _rope_tables(positions):
    half = DF_QK_DIM // 2
    inv_freq = ROPE_THETA ** (-jnp.arange(half, dtype=F32) / half)
    ang = positions.astype(F32).reshape(-1, 1) * inv_freq
    cos, sin = jnp.cos(ang), jnp.sin(ang)
    reps = LANES // DF_QK_DIM
    return (jnp.tile(jnp.concatenate([cos, cos], axis=1), (1, reps)),
            jnp.tile(jnp.concatenate([-sin, sin], axis=1), (1, reps)))


def kernel(x, p, positions, ffn1_pre_g, ffn1_w_gu, ffn1_w_down, ffn1_post_g, mix_pre_g, w_in, diff_lambda, diff_subln_g, w_branch_sb, w_branch_diff, w_out, mix_post_g, ffn2_pre_g, ffn2_w_gu, ffn2_w_down, ffn2_post_g, ple_pre_g, w_ple_gate, w_ple_proj, ple_post_g):
    b, s, d = x.shape
    depth = p.shape[0]
    n = b * s
    h = x.reshape(n, d)
    cos, sin = _rope_tables(positions)
    gain = lambda g, i: g[i].reshape(1, -1).astype(F32)
    wt = lambda w, i: w[i].astype(BF16)
    for i in range(depth):
        h = _ffn(h, gain(ffn1_pre_g, i), wt(ffn1_w_gu, i), wt(ffn1_w_down, i), gain(ffn1_post_g, i))
        sbq, sbk, sbv, dfq, dfk, dfv, gates = _proj(h, gain(mix_pre_g, i), wt(w_in, i), cos, sin)
        o_sb = _sb_attention(sbq, sbk, sbv, b, s)
        lam_init = 0.8 - 0.6 * math.exp(-0.3 * i)
        o_df = _diff_attention(diff_lambda[i].astype(F32), gain(diff_subln_g, i), dfq, dfk, dfv, b, s, lam_init)
        h = _merge(h, o_sb, o_df, gates, wt(w_branch_sb, i), wt(w_branch_diff, i), wt(w_out, i),
                   gain(mix_post_g, i))
        ple = (p[i].reshape(n, -1), gain(ple_pre_g, i), wt(w_ple_gate, i), wt(w_ple_proj, i),
               gain(ple_post_g, i))
        h = _ffn(h, gain(ffn2_pre_g, i), wt(ffn2_w_gu, i), wt(ffn2_w_down, i), gain(ffn2_post_g, i), ple=ple)
    return h.reshape(b, s, d)
```

```python
import functools
import math

import jax
import jax.numpy as jnp
from jax import lax
from jax.experimental import pallas as pl
from jax.experimental.pallas import tpu as pltpu

NORM_EPS = 1e-6
SUBLN_EPS = 1e-5
ROPE_THETA = 10000.0
SB_HEADS = 8
SB_HEAD_DIM = 64
DF_HEADS = 4
DF_QK_DIM = 64
DF_V_DIM = 2 * DF_QK_DIM
SB_W = SB_HEADS * SB_HEAD_DIM
DF_W = DF_HEADS * DF_V_DIM

LANES = 128
LOG2E = math.log2(math.e)
SIGN_BIT = -2 ** 31
FFN_ROWS = 1024
ATTN_BLOCK = 256
SB_GROUPS_PER_STEP = 4
DF_GROUPS_PER_STEP = 4
VMEM_LIMIT_BYTES = 48 * 1024 * 1024

F32 = jnp.float32
BF16 = jnp.bfloat16


def _pick_tile(n, target):
    t = min(n, target)
    while n % t:
        t //= 2
    return t


def _rms(x, g, eps):
    return x * lax.rsqrt(jnp.mean(x * x, axis=-1, keepdims=True) + eps) * g


def _dot(a, b):
    return jnp.dot(a, b, preferred_element_type=F32)


def _dot_nt(a, b):
    return lax.dot_general(a, b, (((1,), (1,)), ((), ())), preferred_element_type=F32)


def _params(*sem):
    return pltpu.CompilerParams(dimension_semantics=sem, vmem_limit_bytes=VMEM_LIMIT_BYTES)


def _ffn_kernel(*refs, tf, with_ple):
    if with_ple:
        (h_ref, pre_g_ref, wgu_ref, wd_ref, post_g_ref,
         p_ref, ple_pre_g_ref, w_gate_ref, w_proj_ref, ple_post_g_ref, out_ref, acc_ref) = refs
    else:
        h_ref, pre_g_ref, wgu_ref, wd_ref, post_g_ref, out_ref, acc_ref = refs
    d_ff = wd_ref.shape[0]
    xn = _rms(h_ref[...], pre_g_ref[...], NORM_EPS).astype(BF16)
    for j in range(d_ff // tf):
        g = _dot(xn, wgu_ref[:, j * tf:(j + 1) * tf])
        u = _dot(xn, wgu_ref[:, d_ff + j * tf:d_ff + (j + 1) * tf])
        a = (g * jax.nn.sigmoid(g) * u).astype(BF16)
        f = _dot(a, wd_ref[j * tf:(j + 1) * tf, :])
        if j == 0:
            acc_ref[...] = f
        else:
            acc_ref[...] += f
    h = h_ref[...] + 0.5 * _rms(acc_ref[...], post_g_ref[...], NORM_EPS)
    if with_ple:
        hn = _rms(h, ple_pre_g_ref[...], NORM_EPS).astype(BF16)
        gate = jax.nn.sigmoid(_dot(hn, w_gate_ref[...]))
        e = _dot(p_ref[...].astype(BF16), w_proj_ref[...]) * gate
        h = h + _rms(e, ple_post_g_ref[...], NORM_EPS)
    out_ref[...] = h


def _resident(shape):
    return pl.BlockSpec(shape, lambda i: (0,) * len(shape), pipeline_mode=pl.Buffered(1))


def _ffn(h, pre_g, w_gu, w_down, post_g, ple=None):
    n, d = h.shape
    d_ff = w_down.shape[0]
    tm = _pick_tile(n, FFN_ROWS)
    tf = 256 if d_ff % 256 == 0 else LANES
    row = lambda i: (i, 0)
    in_specs = [
        pl.BlockSpec((tm, d), row),
        _resident((1, d)),
        _resident(w_gu.shape),
        _resident(w_down.shape),
        _resident((1, d)),
    ]
    args = [h, pre_g, w_gu, w_down, post_g]
    if ple is not None:
        p, ple_pre_g, w_gate, w_proj, ple_post_g = ple
        in_specs += [
            pl.BlockSpec((tm, p.shape[1]), row),
            _resident((1, d)),
            _resident(w_gate.shape),
            _resident(w_proj.shape),
            _resident((1, d)),
        ]
        args += [p, ple_pre_g, w_gate, w_proj, ple_post_g]
    return pl.pallas_call(
        functools.partial(_ffn_kernel, tf=tf, with_ple=ple is not None),
        grid=(n // tm,),
        in_specs=in_specs,
        out_specs=pl.BlockSpec((tm, d), row),
        out_shape=jax.ShapeDtypeStruct((n, d), F32),
        scratch_shapes=[pltpu.VMEM((tm, d), F32)],
        compiler_params=_params("parallel"),
        name="ffn_ple" if ple is not None else "ffn",
    )(*args)


def _rope(y, cos, sin_signed):
    w = y.shape[1]
    reps = w // LANES
    cos = jnp.concatenate([cos] * reps, axis=1)
    sin_signed = jnp.concatenate([sin_signed] * reps, axis=1)
    lane = lax.broadcasted_iota(jnp.int32, y.shape, 1)
    first_half = (lane % DF_QK_DIM) < (DF_QK_DIM // 2)
    half = DF_QK_DIM // 2
    partner = jnp.where(first_half, pltpu.roll(y, w - half, 1), pltpu.roll(y, half, 1))
    return y * cos + partner * sin_signed


def _proj_kernel(h_ref, g_ref, w_ref, cos_ref, sin_ref,
                 sbq_ref, sbk_ref, sbv_ref, dfq_ref, dfk_ref, dfv_ref, gate_ref):
    u = _rms(h_ref[...], g_ref[...], NORM_EPS).astype(BF16)
    cw = SB_W
    sb_scale = LOG2E / math.sqrt(SB_HEAD_DIM)
    df_scale = LOG2E / math.sqrt(DF_QK_DIM)

    def cols(c):
        return _dot(u, w_ref[:, c * cw:(c + 1) * cw])

    sbq_ref[...] = (cols(0) * sb_scale).astype(BF16)
    sbk_ref[...] = cols(1).astype(BF16)
    sbv_ref[...] = cols(2).astype(BF16)
    cos = cos_ref[...]
    sin = sin_ref[...]
    dfq_ref[...] = (_rope(cols(3), cos, sin) * df_scale).astype(BF16)
    dfk_ref[...] = _rope(cols(4), cos, sin).astype(BF16)
    dfv_ref[...] = cols(5).astype(BF16)
    n_gate = gate_ref.shape[1] // cw
    for c in range(n_gate):
        gate_ref[:, c * cw:(c + 1) * cw] = jax.nn.sigmoid(cols(6 + c)).astype(BF16)


def _proj(h, g, w_in, cos, sin):
    n, d = h.shape
    d_in = w_in.shape[1]
    tm = _pick_tile(n, 512)
    row = lambda i: (i, 0)
    const = lambda i: (0, 0)
    n_gate_cols = d_in - 6 * SB_W
    head_out = jax.ShapeDtypeStruct((n, SB_W), BF16)
    return pl.pallas_call(
        _proj_kernel,
        grid=(n // tm,),
        in_specs=[
            pl.BlockSpec((tm, d), row),
            pl.BlockSpec((1, d), const),
            pl.BlockSpec((d, d_in), const),
            pl.BlockSpec((tm, LANES), row),
            pl.BlockSpec((tm, LANES), row),
        ],
        out_specs=[pl.BlockSpec((tm, SB_W), row)] * 6 + [pl.BlockSpec((tm, n_gate_cols), row)],
        out_shape=[head_out] * 6 + [jax.ShapeDtypeStruct((n, n_gate_cols), BF16)],
        compiler_params=_params("parallel"),
        name="proj_in",
    )(h, g, w_in, cos, sin)


def _stack_halves(q):
    lane = lax.broadcasted_iota(jnp.int32, q.shape, 1)
    lo = lane < (LANES // 2)
    zero = jnp.zeros_like(q)
    return jnp.concatenate([jnp.where(lo, q, zero), jnp.where(lo, zero, q)], axis=0)


def _lane_chunks(x):
    return [x[:, c * LANES:(c + 1) * LANES] for c in range(x.shape[1] // LANES)]


def _group(ref, g, rows=None):
    cols = slice(g * LANES, (g + 1) * LANES)
    return ref[:, cols] if rows is None else ref[rows, cols]


def _sb_kernel(q_ref, k_ref, v_ref, o_ref, acc_ref, c_ref, s_ref, w_ref, z_ref, *, blk, groups):
    qi = pl.program_id(2)
    gs = range(groups)
    q2 = [_stack_halves(_group(q_ref, g)) for g in gs]
    r = lax.broadcasted_iota(jnp.int32, (2 * blk, blk), 0) % blk
    c = lax.broadcasted_iota(jnp.int32, (2 * blk, blk), 1)
    strict = c < r
    tri2 = (r >= c).astype(BF16)

    def rows_of(j):
        return pl.ds(pl.multiple_of(j * blk, blk), blk)

    def score(g, j):
        return _dot_nt(q2[g], _group(k_ref, g, rows_of(j)))

    def suffix_sum(z, masked):
        neg_abs = lax.bitcast_convert_type(lax.bitcast_convert_type(z, jnp.int32) | SIGN_BIT, F32)
        sp = jnp.log(1.0 + jnp.exp2(neg_abs)) * LOG2E
        n = jnp.maximum(z, 0.0) + sp
        if masked:
            n = jnp.where(strict, n, 0.0)
        hi = n.astype(BF16)
        lo = (n - hi.astype(F32)).astype(BF16)
        return _dot(jnp.concatenate([hi, lo], axis=1), tri2)

    def store_weights(slot, g, z, suffix, masked):
        wg = jnp.exp2(z - suffix)
        if masked:
            wg = jnp.where(strict, wg, 0.0)
        w_ref[slot, g] = wg.astype(BF16)
        carry = c_ref[g]
        s_ref[slot, g] = jnp.exp2(-carry)
        c_ref[g] = carry + suffix[:, :1]

    def accumulate(slot, g, j):
        acc_ref[g] += s_ref[slot, g] * _dot(w_ref[slot, g], _group(v_ref, g, rows_of(j)))

    def step(cur, j):
        z, suffix = [], []
        for g in gs:
            z.append(z_ref[cur, g])
            suffix.append(suffix_sum(z[g], False))
            z_ref[1 - cur, g] = score(g, jnp.maximum(j - 1, 0))
            accumulate(cur, g, j + 1)
            if g > 0:
                store_weights(1 - cur, g - 1, z[g - 1], suffix[g - 1], False)
        store_weights(1 - cur, groups - 1, z[-1], suffix[-1], False)

    acc_ref[...] = jnp.zeros_like(acc_ref)
    c_ref[...] = jnp.zeros_like(c_ref)
    z = [score(g, qi) for g in gs]
    suffix = [suffix_sum(z[g], True) for g in gs]
    for g in gs:
        store_weights(0, g, z[g], suffix[g], True)
        z_ref[0, g] = score(g, jnp.maximum(qi - 1, 0))

    def pair(t, _):
        j = qi - 1 - 2 * t
        step(0, j)
        step(1, j - 1)
        return 0

    lax.fori_loop(0, qi // 2, pair, 0)

    @pl.when(qi % 2 == 1)
    def _():
        step(0, 0)
        for g in gs:
            accumulate(1, g, 0)

    @pl.when(qi % 2 == 0)
    def _():
        for g in gs:
            accumulate(0, g, 0)

    lane = lax.broadcasted_iota(jnp.int32, (blk, LANES), 1)
    for g in range(groups):
        acc = acc_ref[g]
        o_ref[:, g * LANES:(g + 1) * LANES] = jnp.where(lane < LANES // 2, acc[:blk], acc[blk:]).astype(o_ref.dtype)


def _diff_kernel(lp_ref, g_ref, q_ref, k_ref, v_ref, o_ref, acc_ref, m_ref, l_ref, a_ref, p_ref, z_ref,
                 *, blk, groups, lam_init):
    qi = pl.program_id(2)
    gs = range(groups)
    q2 = [_stack_halves(_group(q_ref, g)) for g in gs]
    r = lax.broadcasted_iota(jnp.int32, (2 * blk, blk), 0) % blk
    c = lax.broadcasted_iota(jnp.int32, (2 * blk, blk), 1)
    causal = c <= r

    def rows_of(j):
        return pl.ds(pl.multiple_of(j * blk, blk), blk)

    def score(g, j):
        return _dot_nt(q2[g], _group(k_ref, g, rows_of(j)))

    def store_probs(slot, g, z, first):
        if first:
            z = jnp.where(causal, z, -jnp.inf)
        zc = _lane_chunks(z)
        zmax = jnp.max(functools.reduce(jnp.maximum, zc), axis=-1, keepdims=True)
        if first:
            m_new = jnp.broadcast_to(zmax, (2 * blk, LANES))
        else:
            m_prev = m_ref[g]
            m_new = jnp.maximum(m_prev, zmax)
        pc = [jnp.exp2(x - m_new) for x in zc]
        psum = functools.reduce(jnp.add, pc)
        p_ref[slot, g] = jnp.concatenate(pc, axis=1).astype(BF16)
        if first:
            l_ref[g] = psum
            a_ref[slot, g] = jnp.ones((2 * blk, LANES), F32)
        else:
            alpha = jnp.exp2(m_prev - m_new)
            l_ref[g] = alpha * l_ref[g] + psum
            a_ref[slot, g] = alpha
        m_ref[g] = m_new

    def accumulate(slot, g, j):
        acc_ref[g] = a_ref[slot, g] * acc_ref[g] + _dot(p_ref[slot, g], _group(v_ref, g, rows_of(j)))

    def step(cur, j):
        for g in gs:
            store_probs(1 - cur, g, z_ref[cur, g], False)
            z_ref[1 - cur, g] = score(g, jnp.maximum(j - 1, 0))
            accumulate(cur, g, j + 1)

    acc_ref[...] = jnp.zeros_like(acc_ref)
    for g in gs:
        store_probs(0, g, score(g, qi), True)
        z_ref[0, g] = score(g, jnp.maximum(qi - 1, 0))

    def pair(t, _):
        j = qi - 1 - 2 * t
        step(0, j)
        step(1, j - 1)
        return 0

    lax.fori_loop(0, qi // 2, pair, 0)

    @pl.when(qi % 2 == 1)
    def _():
        step(0, 0)
        for g in gs:
            accumulate(1, g, 0)

    @pl.when(qi % 2 == 0)
    def _():
        for g in gs:
            accumulate(0, g, 0)

    lp = lp_ref[...]
    lam = (jnp.exp(jnp.sum(lp[0:1] * lp[1:2], axis=-1, keepdims=True))
           - jnp.exp(jnp.sum(lp[2:3] * lp[3:4], axis=-1, keepdims=True)) + lam_init)
    for g in range(groups):
        o = acc_ref[g] / jnp.sum(l_ref[g], axis=-1, keepdims=True)
        o = o[:blk] - lam * o[blk:]
        o_ref[:, g * LANES:(g + 1) * LANES] = (
            _rms(o, g_ref[...], SUBLN_EPS) * (1.0 - lam_init)).astype(o_ref.dtype)


def _attn_specs(s, blk, groups):
    w = groups * LANES
    q_spec = pl.BlockSpec((blk, w), lambda b, h, i: (b * (s // blk) + i, h))
    kv_spec = pl.BlockSpec((s, w), lambda b, h, i: (b, h))
    return q_spec, kv_spec


def _sb_attention(q, k, v, batch, s):
    n, w = q.shape
    blk = _pick_tile(s, ATTN_BLOCK)
    groups = SB_GROUPS_PER_STEP
    q_spec, kv_spec = _attn_specs(s, blk, groups)
    stat = pltpu.VMEM((groups, 2 * blk, LANES), F32)
    tile = (2, groups, 2 * blk, blk)
    return pl.pallas_call(
        functools.partial(_sb_kernel, blk=blk, groups=groups),
        grid=(batch, w // (groups * LANES), s // blk),
        in_specs=[q_spec, kv_spec, kv_spec],
        out_specs=q_spec,
        out_shape=jax.ShapeDtypeStruct((n, w), BF16),
        scratch_shapes=[stat, stat, pltpu.VMEM((2, groups, 2 * blk, LANES), F32),
                        pltpu.VMEM(tile, BF16), pltpu.VMEM(tile, F32)],
        compiler_params=_params("parallel", "parallel", "arbitrary"),
        name="sb_attn",
    )(q, k, v)


def _diff_attention(lp, g, q, k, v, batch, s, lam_init):
    n, w = q.shape
    blk = _pick_tile(s, ATTN_BLOCK)
    groups = DF_GROUPS_PER_STEP
    q_spec, kv_spec = _attn_specs(s, blk, groups)
    const = lambda b, h, i: (0, 0)
    stat = pltpu.VMEM((groups, 2 * blk, LANES), F32)
    tile = (2, groups, 2 * blk, blk)
    return pl.pallas_call(
        functools.partial(_diff_kernel, blk=blk, groups=groups, lam_init=lam_init),
        grid=(batch, w // (groups * LANES), s // blk),
        in_specs=[pl.BlockSpec(lp.shape, const), pl.BlockSpec(g.shape, const), q_spec, kv_spec, kv_spec],
        out_specs=q_spec,
        out_shape=jax.ShapeDtypeStruct((n, w), BF16),
        scratch_shapes=[stat, stat, stat, pltpu.VMEM((2, groups, 2 * blk, LANES), F32),
                        pltpu.VMEM(tile, BF16), pltpu.VMEM(tile, F32)],
        compiler_params=_params("parallel", "parallel", "arbitrary"),
        name="diff_attn",
    )(lp, g, q, k, v)


def _merge_kernel(h_ref, osb_ref, odf_ref, gate_ref, wsb_ref, wdf_ref, wout_ref, g_ref, out_ref):
    d = h_ref.shape[1]
    y_sb = _dot(osb_ref[...], wsb_ref[...])
    y_df = _dot(odf_ref[...], wdf_ref[...])
    merged = gate_ref[:, :d].astype(F32) * y_sb + gate_ref[:, d:].astype(F32) * y_df
    out = _dot(merged.astype(BF16), wout_ref[...])
    out_ref[...] = h_ref[...] + _rms(out, g_ref[...], NORM_EPS)


def _merge(h, o_sb, o_df, gates, w_sb, w_df, w_out, g):
    n, d = h.shape
    tm = _pick_tile(n, 512)
    row = lambda i: (i, 0)
    const = lambda i: (0, 0)
    return pl.pallas_call(
        _merge_kernel,
        grid=(n // tm,),
        in_specs=[
            pl.BlockSpec((tm, d), row),
            pl.BlockSpec((tm, o_sb.shape[1]), row),
            pl.BlockSpec((tm, o_df.shape[1]), row),
            pl.BlockSpec((tm, gates.shape[1]), row),
            pl.BlockSpec(w_sb.shape, const),
            pl.BlockSpec(w_df.shape, const),
            pl.BlockSpec(w_out.shape, const),
            pl.BlockSpec((1, d), const),
        ],
        out_specs=pl.BlockSpec((tm, d), row),
        out_shape=jax.ShapeDtypeStruct((n, d), F32),
        compiler_params=_params("parallel"),
        name="merge_out",
    )(h, o_sb, o_df, gates, w_sb, w_df, w_out, g)


def _rope_tables(positions):
    half = DF_QK_DIM // 2
    inv_freq = ROPE_THETA ** (-jnp.arange(half, dtype=F32) / half)
    ang = positions.astype(F32).reshape(-1, 1) * inv_freq
    cos, sin = jnp.cos(ang), jnp.sin(ang)
    reps = LANES // DF_QK_DIM
    return (jnp.tile(jnp.concatenate([cos, cos], axis=1), (1, reps)),
            jnp.tile(jnp.concatenate([-sin, sin], axis=1), (1, reps)))


def kernel(x, p, positions, ffn1_pre_g, ffn1_w_gu, ffn1_w_down, ffn1_post_g, mix_pre_g, w_in, diff_lambda, diff_subln_g, w_branch_sb, w_branch_diff, w_out, mix_post_g, ffn2_pre_g, ffn2_w_gu, ffn2_w_down, ffn2_post_g, ple_pre_g, w_ple_gate, w_ple_proj, ple_post_g):
    b, s, d = x.shape
    depth = p.shape[0]
    n = b * s
    h = x.reshape(n, d)
    cos, sin = _rope_tables(positions)
    gain = lambda g, i: g[i].reshape(1, -1).astype(F32)
    wt = lambda w, i: w[i].astype(BF16)
    for i in range(depth):
        h = _ffn(h, gain(ffn1_pre_g, i), wt(ffn1_w_gu, i), wt(ffn1_w_down, i), gain(ffn1_post_g, i))
        sbq, sbk, sbv, dfq, dfk, dfv, gates = _proj(h, gain(mix_pre_g, i), wt(w_in, i), cos, sin)
        o_sb = _sb_attention(sbq, sbk, sbv, b, s)
        lam_init = 0.8 - 0.6 * math.exp(-0.3 * i)
        o_df = _diff_attention(diff_lambda[i].astype(F32), gain(diff_subln_g, i), dfq, dfk, dfv, b, s, lam_init)
        h = _merge(h, o_sb, o_df, gates, wt(w_branch_sb, i), wt(w_branch_diff, i), wt(w_out, i),
                   gain(mix_post_g, i))
        ple = (p[i].reshape(n, -1), gain(ple_pre_g, i), wt(w_ple_gate, i), wt(w_ple_proj, i),
               gain(ple_post_g, i))
        h = _ffn(h, gain(ffn2_pre_g, i), wt(ffn2_w_gu, i), wt(ffn2_w_down, i), gain(ffn2_post_g, i), ple=ple)
    return h.reshape(b, s, d)
```

```python
import functools
import math

import jax
import jax.numpy as jnp
from jax import lax
from jax.experimental import pallas as pl
from jax.experimental.pallas import tpu as pltpu

NORM_EPS = 1e-6
SUBLN_EPS = 1e-5
ROPE_THETA = 10000.0
SB_HEADS = 8
SB_HEAD_DIM = 64
DF_HEADS = 4
DF_QK_DIM = 64
DF_V_DIM = 2 * DF_QK_DIM
SB_W = SB_HEADS * SB_HEAD_DIM
DF_W = DF_HEADS * DF_V_DIM

LANES = 128
LOG2E = math.log2(math.e)
SIGN_BIT = -2 ** 31
EXP2_UNDERFLOW = 160.0
NO_BLOCK = 1e6
FFN_ROWS = 1024
ATTN_BLOCK = 256
SB_GROUPS_PER_STEP = 4
DF_GROUPS_PER_STEP = 4
VMEM_LIMIT_BYTES = 48 * 1024 * 1024

F32 = jnp.float32
BF16 = jnp.bfloat16


def _pick_tile(n, target):
    t = min(n, target)
    while n % t:
        t //= 2
    return t


def _rms(x, g, eps):
    return x * lax.rsqrt(jnp.mean(x * x, axis=-1, keepdims=True) + eps) * g


def _dot(a, b):
    return jnp.dot(a, b, preferred_element_type=F32)


def _dot_nt(a, b):
    return lax.dot_general(a, b, (((1,), (1,)), ((), ())), preferred_element_type=F32)


def _params(*sem):
    return pltpu.CompilerParams(dimension_semantics=sem, vmem_limit_bytes=VMEM_LIMIT_BYTES)


def _ffn_kernel(*refs, tf, with_ple):
    if with_ple:
        (h_ref, pre_g_ref, wgu_ref, wd_ref, post_g_ref,
         p_ref, ple_pre_g_ref, w_gate_ref, w_proj_ref, ple_post_g_ref, out_ref, acc_ref) = refs
    else:
        h_ref, pre_g_ref, wgu_ref, wd_ref, post_g_ref, out_ref, acc_ref = refs
    d_ff = wd_ref.shape[0]
    xn = _rms(h_ref[...], pre_g_ref[...], NORM_EPS).astype(BF16)
    for j in range(d_ff // tf):
        g = _dot(xn, wgu_ref[:, j * tf:(j + 1) * tf])
        u = _dot(xn, wgu_ref[:, d_ff + j * tf:d_ff + (j + 1) * tf])
        a = (g * jax.nn.sigmoid(g) * u).astype(BF16)
        f = _dot(a, wd_ref[j * tf:(j + 1) * tf, :])
        if j == 0:
            acc_ref[...] = f
        else:
            acc_ref[...] += f
    h = h_ref[...] + 0.5 * _rms(acc_ref[...], post_g_ref[...], NORM_EPS)
    if with_ple:
        hn = _rms(h, ple_pre_g_ref[...], NORM_EPS).astype(BF16)
        gate = jax.nn.sigmoid(_dot(hn, w_gate_ref[...]))
        e = _dot(p_ref[...].astype(BF16), w_proj_ref[...]) * gate
        h = h + _rms(e, ple_post_g_ref[...], NORM_EPS)
    out_ref[...] = h


def _resident(shape):
    return pl.BlockSpec(shape, lambda i: (0,) * len(shape), pipeline_mode=pl.Buffered(1))


def _ffn(h, pre_g, w_gu, w_down, post_g, ple=None):
    n, d = h.shape
    d_ff = w_down.shape[0]
    tm = _pick_tile(n, FFN_ROWS)
    tf = 256 if d_ff % 256 == 0 else LANES
    row = lambda i: (i, 0)
    in_specs = [
        pl.BlockSpec((tm, d), row),
        _resident((1, d)),
        _resident(w_gu.shape),
        _resident(w_down.shape),
        _resident((1, d)),
    ]
    args = [h, pre_g, w_gu, w_down, post_g]
    if ple is not None:
        p, ple_pre_g, w_gate, w_proj, ple_post_g = ple
        in_specs += [
            pl.BlockSpec((tm, p.shape[1]), row),
            _resident((1, d)),
            _resident(w_gate.shape),
            _resident(w_proj.shape),
            _resident((1, d)),
        ]
        args += [p, ple_pre_g, w_gate, w_proj, ple_post_g]
    return pl.pallas_call(
        functools.partial(_ffn_kernel, tf=tf, with_ple=ple is not None),
        grid=(n // tm,),
        in_specs=in_specs,
        out_specs=pl.BlockSpec((tm, d), row),
        out_shape=jax.ShapeDtypeStruct((n, d), F32),
        scratch_shapes=[pltpu.VMEM((tm, d), F32)],
        compiler_params=_params("parallel"),
        name="ffn_ple" if ple is not None else "ffn",
    )(*args)


def _rope(y, cos, sin_signed):
    w = y.shape[1]
    reps = w // LANES
    cos = jnp.concatenate([cos] * reps, axis=1)
    sin_signed = jnp.concatenate([sin_signed] * reps, axis=1)
    lane = lax.broadcasted_iota(jnp.int32, y.shape, 1)
    first_half = (lane % DF_QK_DIM) < (DF_QK_DIM // 2)
    half = DF_QK_DIM // 2
    partner = jnp.where(first_half, pltpu.roll(y, w - half, 1), pltpu.roll(y, half, 1))
    return y * cos + partner * sin_signed


def _proj_kernel(h_ref, g_ref, w_ref, cos_ref, sin_ref,
                 sbq_ref, sbk_ref, sbv_ref, dfq_ref, dfk_ref, dfv_ref, gate_ref):
    u = _rms(h_ref[...], g_ref[...], NORM_EPS).astype(BF16)
    cw = SB_W
    sb_scale = LOG2E / math.sqrt(SB_HEAD_DIM)
    df_scale = LOG2E / math.sqrt(DF_QK_DIM)

    def cols(c):
        return _dot(u, w_ref[:, c * cw:(c + 1) * cw])

    sbq_ref[...] = (cols(0) * sb_scale).astype(BF16)
    sbk_ref[...] = cols(1).astype(BF16)
    sbv_ref[...] = cols(2).astype(BF16)
    cos = cos_ref[...]
    sin = sin_ref[...]
    dfq_ref[...] = (_rope(cols(3), cos, sin) * df_scale).astype(BF16)
    dfk_ref[...] = _rope(cols(4), cos, sin).astype(BF16)
    dfv_ref[...] = cols(5).astype(BF16)
    n_gate = gate_ref.shape[1] // cw
    for c in range(n_gate):
        gate_ref[:, c * cw:(c + 1) * cw] = jax.nn.sigmoid(cols(6 + c)).astype(BF16)


def _proj(h, g, w_in, cos, sin):
    n, d = h.shape
    d_in = w_in.shape[1]
    tm = _pick_tile(n, 512)
    row = lambda i: (i, 0)
    const = lambda i: (0, 0)
    n_gate_cols = d_in - 6 * SB_W
    head_out = jax.ShapeDtypeStruct((n, SB_W), BF16)
    return pl.pallas_call(
        _proj_kernel,
        grid=(n // tm,),
        in_specs=[
            pl.BlockSpec((tm, d), row),
            pl.BlockSpec((1, d), const),
            pl.BlockSpec((d, d_in), const),
            pl.BlockSpec((tm, LANES), row),
            pl.BlockSpec((tm, LANES), row),
        ],
        out_specs=[pl.BlockSpec((tm, SB_W), row)] * 6 + [pl.BlockSpec((tm, n_gate_cols), row)],
        out_shape=[head_out] * 6 + [jax.ShapeDtypeStruct((n, n_gate_cols), BF16)],
        compiler_params=_params("parallel"),
        name="proj_in",
    )(h, g, w_in, cos, sin)


def _stack_halves(q):
    lane = lax.broadcasted_iota(jnp.int32, q.shape, 1)
    lo = lane < (LANES // 2)
    zero = jnp.zeros_like(q)
    return jnp.concatenate([jnp.where(lo, q, zero), jnp.where(lo, zero, q)], axis=0)


def _lane_chunks(x):
    return [x[:, c * LANES:(c + 1) * LANES] for c in range(x.shape[1] // LANES)]


def _group(ref, g, rows=None):
    cols = slice(g * LANES, (g + 1) * LANES)
    return ref[:, cols] if rows is None else ref[rows, cols]


def _sb_kernel(q_ref, k_ref, v_ref, o_ref, acc_ref, c_ref, s_ref, w_ref, z_ref, *, blk, groups):
    qi = pl.program_id(2)
    gs = range(groups)
    q2 = [_stack_halves(_group(q_ref, g)) for g in gs]
    r = lax.broadcasted_iota(jnp.int32, (2 * blk, blk), 0) % blk
    c = lax.broadcasted_iota(jnp.int32, (2 * blk, blk), 1)
    strict = c < r
    tri2 = (r >= c).astype(BF16)

    def rows_of(j):
        return pl.ds(pl.multiple_of(j * blk, blk), blk)

    def score(g, j):
        return _dot_nt(q2[g], _group(k_ref, g, rows_of(j)))

    def suffix_sum(z, masked):
        neg_abs = lax.bitcast_convert_type(lax.bitcast_convert_type(z, jnp.int32) | SIGN_BIT, F32)
        sp = jnp.log(1.0 + jnp.exp2(neg_abs)) * LOG2E
        n = jnp.maximum(z, 0.0) + sp
        if masked:
            n = jnp.where(strict, n, 0.0)
        hi = n.astype(BF16)
        lo = (n - hi.astype(F32)).astype(BF16)
        return _dot(jnp.concatenate([hi, lo], axis=1), tri2)

    def store_weights(slot, g, z, suffix, masked):
        wg = jnp.exp2(z - suffix)
        if masked:
            wg = jnp.where(strict, wg, 0.0)
        w_ref[slot, g] = wg.astype(BF16)
        carry = c_ref[g]
        s_ref[slot, g] = jnp.exp2(-carry)
        c_ref[g] = carry + suffix[:, :1]

    def accumulate(slot, g, j):
        acc_ref[g] += s_ref[slot, g] * _dot(w_ref[slot, g], _group(v_ref, g, rows_of(j)))

    def step(cur, j):
        z, suffix = [], []
        for g in gs:
            z.append(z_ref[cur, g])
            suffix.append(suffix_sum(z[g], False))
            z_ref[1 - cur, g] = score(g, jnp.maximum(j - 1, 0))
            accumulate(cur, g, j + 1)
            if g > 0:
                store_weights(1 - cur, g - 1, z[g - 1], suffix[g - 1], False)
        store_weights(1 - cur, groups - 1, z[-1], suffix[-1], False)

    acc_ref[...] = jnp.zeros_like(acc_ref)
    c_ref[...] = jnp.zeros_like(c_ref)
    no_block = jnp.where(qi == 0, NO_BLOCK, 0.0)
    j1 = jnp.maximum(qi - 1, 0)
    z0 = [score(g, qi) for g in gs]
    z1 = [score(g, j1) for g in gs]
    for g in gs:
        suffix0 = suffix_sum(z0[g], True)
        suffix1 = suffix_sum(z1[g], False)
        store_weights(0, g, z0[g], suffix0, True)
        c_ref[g] += no_block
        accumulate(0, g, qi)
        store_weights(1, g, z1[g], suffix1, False)
        z_ref[1, g] = score(g, jnp.maximum(qi - 2, 0))

    def more(state):
        t, carry_min = state
        return jnp.logical_and(t < qi, carry_min < EXP2_UNDERFLOW)

    def body(state):
        t, _ = state
        step(t % 2, qi - 1 - t)
        return t + 1, jnp.min(c_ref[...])

    t_end, _ = lax.while_loop(more, body, (jnp.int32(1), jnp.min(c_ref[...])))
    for g in gs:
        accumulate(t_end % 2, g, jnp.maximum(qi - t_end, 0))

    lane = lax.broadcasted_iota(jnp.int32, (blk, LANES), 1)
    for g in range(groups):
        acc = acc_ref[g]
        o_ref[:, g * LANES:(g + 1) * LANES] = jnp.where(lane < LANES // 2, acc[:blk], acc[blk:]).astype(o_ref.dtype)


def _diff_kernel(lp_ref, g_ref, q_ref, k_ref, v_ref, o_ref, acc_ref, m_ref, l_ref, a_ref, p_ref, z_ref,
                 *, blk, groups, lam_init):
    qi = pl.program_id(2)
    gs = range(groups)
    q2 = [_stack_halves(_group(q_ref, g)) for g in gs]
    r = lax.broadcasted_iota(jnp.int32, (2 * blk, blk), 0) % blk
    c = lax.broadcasted_iota(jnp.int32, (2 * blk, blk), 1)
    causal = c <= r

    def rows_of(j):
        return pl.ds(pl.multiple_of(j * blk, blk), blk)

    def score(g, j):
        return _dot_nt(q2[g], _group(k_ref, g, rows_of(j)))

    def store_probs(slot, g, z, first):
        if first:
            z = jnp.where(causal, z, -jnp.inf)
        zc = _lane_chunks(z)
        zmax = jnp.max(functools.reduce(jnp.maximum, zc), axis=-1, keepdims=True)
        if first:
            m_new = jnp.broadcast_to(zmax, (2 * blk, LANES))
        else:
            m_prev = m_ref[g]
            m_new = jnp.maximum(m_prev, zmax)
        pc = [jnp.exp2(x - m_new) for x in zc]
        psum = functools.reduce(jnp.add, pc)
        p_ref[slot, g] = jnp.concatenate(pc, axis=1).astype(BF16)
        if first:
            l_ref[g] = psum
            a_ref[slot, g] = jnp.ones((2 * blk, LANES), F32)
        else:
            alpha = jnp.exp2(m_prev - m_new)
            l_ref[g] = alpha * l_ref[g] + psum
            a_ref[slot, g] = alpha
        m_ref[g] = m_new

    def accumulate(slot, g, j):
        acc_ref[g] = a_ref[slot, g] * acc_ref[g] + _dot(p_ref[slot, g], _group(v_ref, g, rows_of(j)))

    def step(cur, j):
        for g in gs:
            store_probs(1 - cur, g, z_ref[cur, g], False)
            z_ref[1 - cur, g] = score(g, jnp.maximum(j - 1, 0))
            accumulate(cur, g, j + 1)

    acc_ref[...] = jnp.zeros_like(acc_ref)
    for g in gs:
        store_probs(0, g, score(g, qi), True)
        z_ref[0, g] = score(g, jnp.maximum(qi - 1, 0))

    def pair(t, _):
        j = qi - 1 - 2 * t
        step(0, j)
        step(1, j - 1)
        return 0

    lax.fori_loop(0, qi // 2, pair, 0)

    @pl.when(qi % 2 == 1)
    def _():
        step(0, 0)
        for g in gs:
            accumulate(1, g, 0)

    @pl.when(qi % 2 == 0)
    def _():
        for g in gs:
            accumulate(0, g, 0)

    lp = lp_ref[...]
    lam = (jnp.exp(jnp.sum(lp[0:1] * lp[1:2], axis=-1, keepdims=True))
           - jnp.exp(jnp.sum(lp[2:3] * lp[3:4], axis=-1, keepdims=True)) + lam_init)
    for g in range(groups):
        o = acc_ref[g] / jnp.sum(l_ref[g], axis=-1, keepdims=True)
        o = o[:blk] - lam * o[blk:]
        o_ref[:, g * LANES:(g + 1) * LANES] = (
            _rms(o, g_ref[...], SUBLN_EPS) * (1.0 - lam_init)).astype(o_ref.dtype)


def _attn_specs(s, blk, groups):
    w = groups * LANES
    q_spec = pl.BlockSpec((blk, w), lambda b, h, i: (b * (s // blk) + i, h))
    kv_spec = pl.BlockSpec((s, w), lambda b, h, i: (b, h))
    return q_spec, kv_spec


def _sb_attention(q, k, v, batch, s):
    n, w = q.shape
    blk = _pick_tile(s, ATTN_BLOCK)
    groups = SB_GROUPS_PER_STEP
    q_spec, kv_spec = _attn_specs(s, blk, groups)
    stat = pltpu.VMEM((groups, 2 * blk, LANES), F32)
    tile = (2, groups, 2 * blk, blk)
    return pl.pallas_call(
        functools.partial(_sb_kernel, blk=blk, groups=groups),
        grid=(batch, w // (groups * LANES), s // blk),
        in_specs=[q_spec, kv_spec, kv_spec],
        out_specs=q_spec,
        out_shape=jax.ShapeDtypeStruct((n, w), BF16),
        scratch_shapes=[stat, stat, pltpu.VMEM((2, groups, 2 * blk, LANES), F32),
                        pltpu.VMEM(tile, BF16), pltpu.VMEM(tile, F32)],
        compiler_params=_params("parallel", "parallel", "arbitrary"),
        name="sb_attn",
    )(q, k, v)


def _diff_attention(lp, g, q, k, v, batch, s, lam_init):
    n, w = q.shape
    blk = _pick_tile(s, ATTN_BLOCK)
    groups = DF_GROUPS_PER_STEP
    q_spec, kv_spec = _attn_specs(s, blk, groups)
    const = lambda b, h, i: (0, 0)
    stat = pltpu.VMEM((groups, 2 * blk, LANES), F32)
    tile = (2, groups, 2 * blk, blk)
    return pl.pallas_call(
        functools.partial(_diff_kernel, blk=blk, groups=groups, lam_init=lam_init),
        grid=(batch, w // (groups * LANES), s // blk),
        in_specs=[pl.BlockSpec(lp.shape, const), pl.BlockSpec(g.shape, const), q_spec, kv_spec, kv_spec],
        out_specs=q_spec,
        out_shape=jax.ShapeDtypeStruct((n, w), BF16),
        scratch_shapes=[stat, stat, stat, pltpu.VMEM((2, groups, 2 * blk, LANES), F32),
                        pltpu.VMEM(tile, BF16), pltpu.VMEM(tile, F32)],
        compiler_params=_params("parallel", "parallel", "arbitrary"),
        name="diff_attn",
    )(lp, g, q, k, v)


def _merge_kernel(h_ref, osb_ref, odf_ref, gate_ref, wsb_ref, wdf_ref, wout_ref, g_ref, out_ref):
    d = h_ref.shape[1]
    y_sb = _dot(osb_ref[...], wsb_ref[...])
    y_df = _dot(odf_ref[...], wdf_ref[...])
    merged = gate_ref[:, :d].astype(F32) * y_sb + gate_ref[:, d:].astype(F32) * y_df
    out = _dot(merged.astype(BF16), wout_ref[...])
    out_ref[...] = h_ref[...] + _rms(out, g_ref[...], NORM_EPS)


def _merge(h, o_sb, o_df, gates, w_sb, w_df, w_out, g):
    n, d = h.shape
    tm = _pick_tile(n, 512)
    row = lambda i: (i, 0)
    const = lambda i: (0, 0)
    return pl.pallas_call(
        _merge_kernel,
        grid=(n // tm,),
        in_specs=[
            pl.BlockSpec((tm, d), row),
            pl.BlockSpec((tm, o_sb.shape[1]), row),
            pl.BlockSpec((tm, o_df.shape[1]), row),
            pl.BlockSpec((tm, gates.shape[1]), row),
            pl.BlockSpec(w_sb.shape, const),
            pl.BlockSpec(w_df.shape, const),
            pl.BlockSpec(w_out.shape, const),
            pl.BlockSpec((1, d), const),
        ],
        out_specs=pl.BlockSpec((tm, d), row),
        out_shape=jax.ShapeDtypeStruct((n, d), F32),
        compiler_params=_params("parallel"),
        name="merge_out",
    )(h, o_sb, o_df, gates, w_sb, w_df, w_out, g)


def _rope_tables(positions):
    half = DF_QK_DIM // 2
    inv_freq = ROPE_THETA ** (-jnp.arange(half, dtype=F32) / half)
    ang = positions.astype(F32).reshape(-1, 1) * inv_freq
    cos, sin = jnp.cos(ang), jnp.sin(ang)
    reps = LANES // DF_QK_DIM
    return (jnp.tile(jnp.concatenate([cos, cos], axis=1), (1, reps)),
            jnp.tile(jnp.concatenate([-sin, sin], axis=1), (1, reps)))


def kernel(x, p, positions, ffn1_pre_g, ffn1_w_gu, ffn1_w_down, ffn1_post_g, mix_pre_g, w_in, diff_lambda, diff_subln_g, w_branch_sb, w_branch_diff, w_out, mix_post_g, ffn2_pre_g, ffn2_w_gu, ffn2_w_down, ffn2_post_g, ple_pre_g, w_ple_gate, w_ple_proj, ple_post_g):
    b, s, d = x.shape
    depth = p.shape[0]
    n = b * s
    h = x.reshape(n, d)
    cos, sin = _rope_tables(positions)
    gain = lambda g, i: g[i].reshape(1, -1).astype(F32)
    wt = lambda w, i: w[i].astype(BF16)
    for i in range(depth):
        h = _ffn(h, gain(ffn1_pre_g, i), wt(ffn1_w_gu, i), wt(ffn1_w_down, i), gain(ffn1_post_g, i))
        sbq, sbk, sbv, dfq, dfk, dfv, gates = _proj(h, gain(mix_pre_g, i), wt(w_in, i), cos, sin)
        o_sb = _sb_attention(sbq, sbk, sbv, b, s)
        lam_init = 0.8 - 0.6 * math.exp(-0.3 * i)
        o_df = _diff_attention(diff_lambda[i].astype(F32), gain(diff_subln_g, i), dfq, dfk, dfv, b, s, lam_init)
        h = _merge(h, o_sb, o_df, gates, wt(w_branch_sb, i), wt(w_branch_diff, i), wt(w_out, i),
                   gain(mix_post_g, i))
        ple = (p[i].reshape(n, -1), gain(ple_pre_g, i), wt(w_ple_gate, i), wt(w_ple_proj, i),
               gain(ple_post_g, i))
        h = _ffn(h, gain(ffn2_pre_g, i), wt(ffn2_w_gu, i), wt(ffn2_w_down, i), gain(ffn2_post_g, i), ple=ple)
    return h.reshape(b, s, d)
```

```python
import functools
import math

import jax
import jax.numpy as jnp
from jax import lax
from jax.experimental import pallas as pl
from jax.experimental.pallas import tpu as pltpu

NORM_EPS = 1e-6
SUBLN_EPS = 1e-5
ROPE_THETA = 10000.0
SB_HEADS = 8
SB_HEAD_DIM = 64
DF_HEADS = 4
DF_QK_DIM = 64
DF_V_DIM = 2 * DF_QK_DIM
SB_W = SB_HEADS * SB_HEAD_DIM
DF_W = DF_HEADS * DF_V_DIM

LANES = 128
LOG2E = math.log2(math.e)
SIGN_BIT = -2 ** 31
EXP2_UNDERFLOW = 160.0
NO_BLOCK = 1e6
FFN_ROWS = 1024
ATTN_BLOCK = 256
SB_GROUPS_PER_STEP = 4
DF_GROUPS_PER_STEP = 4
VMEM_LIMIT_BYTES = 48 * 1024 * 1024

F32 = jnp.float32
BF16 = jnp.bfloat16


def _pick_tile(n, target):
    t = min(n, target)
    while n % t:
        t //= 2
    return t


def _rms(x, g, eps):
    return x * lax.rsqrt(jnp.mean(x * x, axis=-1, keepdims=True) + eps) * g


def _dot(a, b):
    return jnp.dot(a, b, preferred_element_type=F32)


def _dot_nt(a, b):
    return lax.dot_general(a, b, (((1,), (1,)), ((), ())), preferred_element_type=F32)


def _params(*sem):
    return pltpu.CompilerParams(dimension_semantics=sem, vmem_limit_bytes=VMEM_LIMIT_BYTES)


def _ffn_kernel(*refs, tf, with_ple):
    if with_ple:
        (h_ref, pre_g_ref, wgu_ref, wd_ref, post_g_ref,
         p_ref, ple_pre_g_ref, w_gate_ref, w_proj_ref, ple_post_g_ref, out_ref, acc_ref) = refs
    else:
        h_ref, pre_g_ref, wgu_ref, wd_ref, post_g_ref, out_ref, acc_ref = refs
    d_ff = wd_ref.shape[0]
    xn = _rms(h_ref[...], pre_g_ref[...], NORM_EPS).astype(BF16)
    for j in range(d_ff // tf):
        g = _dot(xn, wgu_ref[:, j * tf:(j + 1) * tf])
        u = _dot(xn, wgu_ref[:, d_ff + j * tf:d_ff + (j + 1) * tf])
        a = (g * jax.nn.sigmoid(g) * u).astype(BF16)
        f = _dot(a, wd_ref[j * tf:(j + 1) * tf, :])
        if j == 0:
            acc_ref[...] = f
        else:
            acc_ref[...] += f
    h = h_ref[...] + 0.5 * _rms(acc_ref[...], post_g_ref[...], NORM_EPS)
    if with_ple:
        hn = _rms(h, ple_pre_g_ref[...], NORM_EPS).astype(BF16)
        gate = jax.nn.sigmoid(_dot(hn, w_gate_ref[...]))
        e = _dot(p_ref[...].astype(BF16), w_proj_ref[...]) * gate
        h = h + _rms(e, ple_post_g_ref[...], NORM_EPS)
    out_ref[...] = h


def _resident(shape):
    return pl.BlockSpec(shape, lambda i: (0,) * len(shape), pipeline_mode=pl.Buffered(1))


def _ffn(h, pre_g, w_gu, w_down, post_g, ple=None):
    n, d = h.shape
    d_ff = w_down.shape[0]
    tm = _pick_tile(n, FFN_ROWS)
    tf = 256 if d_ff % 256 == 0 else LANES
    row = lambda i: (i, 0)
    in_specs = [
        pl.BlockSpec((tm, d), row),
        _resident((1, d)),
        _resident(w_gu.shape),
        _resident(w_down.shape),
        _resident((1, d)),
    ]
    args = [h, pre_g, w_gu, w_down, post_g]
    if ple is not None:
        p, ple_pre_g, w_gate, w_proj, ple_post_g = ple
        in_specs += [
            pl.BlockSpec((tm, p.shape[1]), row),
            _resident((1, d)),
            _resident(w_gate.shape),
            _resident(w_proj.shape),
            _resident((1, d)),
        ]
        args += [p, ple_pre_g, w_gate, w_proj, ple_post_g]
    return pl.pallas_call(
        functools.partial(_ffn_kernel, tf=tf, with_ple=ple is not None),
        grid=(n // tm,),
        in_specs=in_specs,
        out_specs=pl.BlockSpec((tm, d), row),
        out_shape=jax.ShapeDtypeStruct((n, d), F32),
        scratch_shapes=[pltpu.VMEM((tm, d), F32)],
        compiler_params=_params("parallel"),
        name="ffn_ple" if ple is not None else "ffn",
    )(*args)


def _rope(y, cos, sin_signed):
    w = y.shape[1]
    reps = w // LANES
    cos = jnp.concatenate([cos] * reps, axis=1)
    sin_signed = jnp.concatenate([sin_signed] * reps, axis=1)
    lane = lax.broadcasted_iota(jnp.int32, y.shape, 1)
    first_half = (lane % DF_QK_DIM) < (DF_QK_DIM // 2)
    half = DF_QK_DIM // 2
    partner = jnp.where(first_half, pltpu.roll(y, w - half, 1), pltpu.roll(y, half, 1))
    return y * cos + partner * sin_signed


def _proj_kernel(h_ref, g_ref, w_ref, cos_ref, sin_ref,
                 sbq_ref, sbk_ref, sbv_ref, dfq_ref, dfk_ref, dfv_ref, gate_ref):
    u = _rms(h_ref[...], g_ref[...], NORM_EPS).astype(BF16)
    cw = SB_W
    sb_scale = LOG2E / math.sqrt(SB_HEAD_DIM)
    df_scale = LOG2E / math.sqrt(DF_QK_DIM)

    def cols(c):
        return _dot(u, w_ref[:, c * cw:(c + 1) * cw])

    sbq_ref[...] = (cols(0) * sb_scale).astype(BF16)
    sbk_ref[...] = cols(1).astype(BF16)
    sbv_ref[...] = cols(2).astype(BF16)
    cos = cos_ref[...]
    sin = sin_ref[...]
    dfq_ref[...] = (_rope(cols(3), cos, sin) * df_scale).astype(BF16)
    dfk_ref[...] = _rope(cols(4), cos, sin).astype(BF16)
    dfv_ref[...] = cols(5).astype(BF16)
    n_gate = gate_ref.shape[1] // cw
    for c in range(n_gate):
        gate_ref[:, c * cw:(c + 1) * cw] = jax.nn.sigmoid(cols(6 + c)).astype(BF16)


def _proj(h, g, w_in, cos, sin):
    n, d = h.shape
    d_in = w_in.shape[1]
    tm = _pick_tile(n, 512)
    row = lambda i: (i, 0)
    const = lambda i: (0, 0)
    n_gate_cols = d_in - 6 * SB_W
    head_out = jax.ShapeDtypeStruct((n, SB_W), BF16)
    return pl.pallas_call(
        _proj_kernel,
        grid=(n // tm,),
        in_specs=[
            pl.BlockSpec((tm, d), row),
            pl.BlockSpec((1, d), const),
            pl.BlockSpec((d, d_in), const),
            pl.BlockSpec((tm, LANES), row),
            pl.BlockSpec((tm, LANES), row),
        ],
        out_specs=[pl.BlockSpec((tm, SB_W), row)] * 6 + [pl.BlockSpec((tm, n_gate_cols), row)],
        out_shape=[head_out] * 6 + [jax.ShapeDtypeStruct((n, n_gate_cols), BF16)],
        compiler_params=_params("parallel"),
        name="proj_in",
    )(h, g, w_in, cos, sin)


def _stack_halves(q):
    lane = lax.broadcasted_iota(jnp.int32, q.shape, 1)
    lo = lane < (LANES // 2)
    zero = jnp.zeros_like(q)
    return jnp.concatenate([jnp.where(lo, q, zero), jnp.where(lo, zero, q)], axis=0)


def _group(ref, g, rows=None):
    cols = slice(g * LANES, (g + 1) * LANES)
    return ref[:, cols] if rows is None else ref[rows, cols]


def _sb_kernel(q_ref, k_ref, v_ref, o_ref, acc_ref, c_ref, s_ref, w_ref, z_ref, *, blk, groups):
    qi = pl.program_id(2)
    gs = range(groups)
    q2 = [_stack_halves(_group(q_ref, g)) for g in gs]
    r = lax.broadcasted_iota(jnp.int32, (2 * blk, blk), 0) % blk
    c = lax.broadcasted_iota(jnp.int32, (2 * blk, blk), 1)
    strict = c < r
    tri2 = (r >= c).astype(BF16)

    def rows_of(j):
        return pl.ds(pl.multiple_of(j * blk, blk), blk)

    def score(g, j):
        return _dot_nt(q2[g], _group(k_ref, g, rows_of(j)))

    def suffix_sum(z, masked):
        neg_abs = lax.bitcast_convert_type(lax.bitcast_convert_type(z, jnp.int32) | SIGN_BIT, F32)
        sp = jnp.log(1.0 + jnp.exp2(neg_abs)) * LOG2E
        n = jnp.maximum(z, 0.0) + sp
        if masked:
            n = jnp.where(strict, n, 0.0)
        hi = n.astype(BF16)
        lo = (n - hi.astype(F32)).astype(BF16)
        return _dot(jnp.concatenate([hi, lo], axis=1), tri2)

    def store_weights(slot, g, z, suffix, masked):
        wg = jnp.exp2(z - suffix)
        if masked:
            wg = jnp.where(strict, wg, 0.0)
        w_ref[slot, g] = wg.astype(BF16)
        carry = c_ref[g]
        s_ref[slot, g] = jnp.exp2(-carry)
        c_ref[g] = carry + suffix[:, :1]

    def accumulate(slot, g, j):
        acc_ref[g] += s_ref[slot, g] * _dot(w_ref[slot, g], _group(v_ref, g, rows_of(j)))

    def step(cur, j):
        z, suffix = [], []
        for g in gs:
            z.append(z_ref[cur, g])
            suffix.append(suffix_sum(z[g], False))
            z_ref[1 - cur, g] = score(g, jnp.maximum(j - 1, 0))
            accumulate(cur, g, j + 1)
            if g > 0:
                store_weights(1 - cur, g - 1, z[g - 1], suffix[g - 1], False)
        store_weights(1 - cur, groups - 1, z[-1], suffix[-1], False)

    acc_ref[...] = jnp.zeros_like(acc_ref)
    c_ref[...] = jnp.zeros_like(c_ref)
    no_block = jnp.where(qi == 0, NO_BLOCK, 0.0)
    j1 = jnp.maximum(qi - 1, 0)
    z0 = [score(g, qi) for g in gs]
    z1 = [score(g, j1) for g in gs]
    for g in gs:
        suffix0 = suffix_sum(z0[g], True)
        suffix1 = suffix_sum(z1[g], False)
        store_weights(0, g, z0[g], suffix0, True)
        c_ref[g] += no_block
        accumulate(0, g, qi)
        store_weights(1, g, z1[g], suffix1, False)
        z_ref[1, g] = score(g, jnp.maximum(qi - 2, 0))

    def more(state):
        t, carry_min = state
        return jnp.logical_and(t < qi, carry_min < EXP2_UNDERFLOW)

    def body(state):
        t, _ = state
        step(t % 2, qi - 1 - t)
        return t + 1, jnp.min(c_ref[...])

    t_end, _ = lax.while_loop(more, body, (jnp.int32(1), jnp.min(c_ref[...])))
    for g in gs:
        accumulate(t_end % 2, g, jnp.maximum(qi - t_end, 0))

    lane = lax.broadcasted_iota(jnp.int32, (blk, LANES), 1)
    for g in range(groups):
        acc = acc_ref[g]
        o_ref[:, g * LANES:(g + 1) * LANES] = jnp.where(lane < LANES // 2, acc[:blk], acc[blk:]).astype(o_ref.dtype)


def _diff_kernel(lp_ref, g_ref, q_ref, k_ref, v_ref, o_ref, acc_ref, m_ref, l_ref, a_ref, p_ref, z_ref,
                 *, blk, groups, lam_init):
    qi = pl.program_id(2)
    gs = range(groups)
    q2 = [_stack_halves(_group(q_ref, g)) for g in gs]
    key = lax.broadcasted_iota(jnp.int32, (blk, 2 * blk), 0)
    qry = lax.broadcasted_iota(jnp.int32, (blk, 2 * blk), 1) % blk
    causal = key <= qry

    def rows_of(j):
        return pl.ds(pl.multiple_of(j * blk, blk), blk)

    def score(g, j):
        return _dot_nt(_group(k_ref, g, rows_of(j)), q2[g])

    def store_probs(slot, g, z, first):
        if first:
            z = jnp.where(causal, z, -jnp.inf)
        zmax = jnp.max(z, axis=0, keepdims=True)
        if first:
            m_new = zmax
        else:
            m_prev = m_ref[g]
            m_new = jnp.maximum(m_prev, zmax)
        p = jnp.exp2(z - m_new)
        psum = jnp.sum(p, axis=0, keepdims=True)
        p_ref[slot, g] = p.astype(BF16)
        if first:
            l_ref[g] = psum
            a_ref[slot, g] = jnp.ones((1, 2 * blk), F32)
        else:
            alpha = jnp.exp2(m_prev - m_new)
            l_ref[g] = alpha * l_ref[g] + psum
            a_ref[slot, g] = alpha
        m_ref[g] = m_new

    def accumulate(slot, g, j):
        pv = lax.dot_general(_group(v_ref, g, rows_of(j)), p_ref[slot, g], (((0,), (0,)), ((), ())),
                             preferred_element_type=F32)
        acc_ref[g] = a_ref[slot, g] * acc_ref[g] + pv

    def step(cur, j):
        for g in gs:
            store_probs(1 - cur, g, z_ref[cur, g], False)
            z_ref[1 - cur, g] = score(g, jnp.maximum(j - 1, 0))
            accumulate(cur, g, j + 1)

    acc_ref[...] = jnp.zeros_like(acc_ref)
    for g in gs:
        store_probs(0, g, score(g, qi), True)
        z_ref[0, g] = score(g, jnp.maximum(qi - 1, 0))

    def pair(t, _):
        j = qi - 1 - 2 * t
        step(0, j)
        step(1, j - 1)
        return 0

    lax.fori_loop(0, qi // 2, pair, 0)

    @pl.when(qi % 2 == 1)
    def _():
        step(0, 0)
        for g in gs:
            accumulate(1, g, 0)

    @pl.when(qi % 2 == 0)
    def _():
        for g in gs:
            accumulate(0, g, 0)

    lp = lp_ref[...]
    lam = (jnp.exp(jnp.sum(lp[0:1] * lp[1:2], axis=-1, keepdims=True))
           - jnp.exp(jnp.sum(lp[2:3] * lp[3:4], axis=-1, keepdims=True)) + lam_init)
    for g in range(groups):
        ot = acc_ref[g] / l_ref[g]
        o = (ot[:, :blk] - lam * ot[:, blk:]).T
        o_ref[:, g * LANES:(g + 1) * LANES] = (
            _rms(o, g_ref[...], SUBLN_EPS) * (1.0 - lam_init)).astype(o_ref.dtype)


def _attn_specs(s, blk, groups):
    w = groups * LANES
    q_spec = pl.BlockSpec((blk, w), lambda b, h, i: (b * (s // blk) + i, h))
    kv_spec = pl.BlockSpec((s, w), lambda b, h, i: (b, h))
    return q_spec, kv_spec


def _sb_attention(q, k, v, batch, s):
    n, w = q.shape
    blk = _pick_tile(s, ATTN_BLOCK)
    groups = SB_GROUPS_PER_STEP
    q_spec, kv_spec = _attn_specs(s, blk, groups)
    stat = pltpu.VMEM((groups, 2 * blk, LANES), F32)
    tile = (2, groups, 2 * blk, blk)
    return pl.pallas_call(
        functools.partial(_sb_kernel, blk=blk, groups=groups),
        grid=(batch, w // (groups * LANES), s // blk),
        in_specs=[q_spec, kv_spec, kv_spec],
        out_specs=q_spec,
        out_shape=jax.ShapeDtypeStruct((n, w), BF16),
        scratch_shapes=[stat, stat, pltpu.VMEM((2, groups, 2 * blk, LANES), F32),
                        pltpu.VMEM(tile, BF16), pltpu.VMEM(tile, F32)],
        compiler_params=_params("parallel", "parallel", "arbitrary"),
        name="sb_attn",
    )(q, k, v)


def _diff_attention(lp, g, q, k, v, batch, s, lam_init):
    n, w = q.shape
    blk = _pick_tile(s, ATTN_BLOCK)
    groups = DF_GROUPS_PER_STEP
    q_spec, kv_spec = _attn_specs(s, blk, groups)
    const = lambda b, h, i: (0, 0)
    stat = pltpu.VMEM((groups, 1, 2 * blk), F32)
    tile = (2, groups, blk, 2 * blk)
    return pl.pallas_call(
        functools.partial(_diff_kernel, blk=blk, groups=groups, lam_init=lam_init),
        grid=(batch, w // (groups * LANES), s // blk),
        in_specs=[pl.BlockSpec(lp.shape, const), pl.BlockSpec(g.shape, const), q_spec, kv_spec, kv_spec],
        out_specs=q_spec,
        out_shape=jax.ShapeDtypeStruct((n, w), BF16),
        scratch_shapes=[pltpu.VMEM((groups, LANES, 2 * blk), F32), stat, stat,
                        pltpu.VMEM((2, groups, 1, 2 * blk), F32),
                        pltpu.VMEM(tile, BF16), pltpu.VMEM(tile, F32)],
        compiler_params=_params("parallel", "parallel", "arbitrary"),
        name="diff_attn",
    )(lp, g, q, k, v)


def _merge_kernel(h_ref, osb_ref, odf_ref, gate_ref, wsb_ref, wdf_ref, wout_ref, g_ref, out_ref):
    d = h_ref.shape[1]
    y_sb = _dot(osb_ref[...], wsb_ref[...])
    y_df = _dot(odf_ref[...], wdf_ref[...])
    merged = gate_ref[:, :d].astype(F32) * y_sb + gate_ref[:, d:].astype(F32) * y_df
    out = _dot(merged.astype(BF16), wout_ref[...])
    out_ref[...] = h_ref[...] + _rms(out, g_ref[...], NORM_EPS)


def _merge(h, o_sb, o_df, gates, w_sb, w_df, w_out, g):
    n, d = h.shape
    tm = _pick_tile(n, 512)
    row = lambda i: (i, 0)
    const = lambda i: (0, 0)
    return pl.pallas_call(
        _merge_kernel,
        grid=(n // tm,),
        in_specs=[
            pl.BlockSpec((tm, d), row),
            pl.BlockSpec((tm, o_sb.shape[1]), row),
            pl.BlockSpec((tm, o_df.shape[1]), row),
            pl.BlockSpec((tm, gates.shape[1]), row),
            pl.BlockSpec(w_sb.shape, const),
            pl.BlockSpec(w_df.shape, const),
            pl.BlockSpec(w_out.shape, const),
            pl.BlockSpec((1, d), const),
        ],
        out_specs=pl.BlockSpec((tm, d), row),
        out_shape=jax.ShapeDtypeStruct((n, d), F32),
        compiler_params=_params("parallel"),
        name="merge_out",
    )(h, o_sb, o_df, gates, w_sb, w_df, w_out, g)


def _rope_tables(positions):
    half = DF_QK_DIM // 2
    inv_freq = ROPE_THETA ** (-jnp.arange(half, dtype=F32) / half)
    ang = positions.astype(F32).reshape(-1, 1) * inv_freq
    cos, sin = jnp.cos(ang), jnp.sin(ang)
    reps = LANES // DF_QK_DIM
    return (jnp.tile(jnp.concatenate([cos, cos], axis=1), (1, reps)),
            jnp.tile(jnp.concatenate([-sin, sin], axis=1), (1, reps)))


def kernel(x, p, positions, ffn1_pre_g, ffn1_w_gu, ffn1_w_down, ffn1_post_g, mix_pre_g, w_in, diff_lambda, diff_subln_g, w_branch_sb, w_branch_diff, w_out, mix_post_g, ffn2_pre_g, ffn2_w_gu, ffn2_w_down, ffn2_post_g, ple_pre_g, w_ple_gate, w_ple_proj, ple_post_g):
    b, s, d = x.shape
    depth = p.shape[0]
    n = b * s
    h = x.reshape(n, d)
    cos, sin = _rope_tables(positions)
    gain = lambda g, i: g[i].reshape(1, -1).astype(F32)
    wt = lambda w, i: w[i].astype(BF16)
    for i in range(depth):
        h = _ffn(h, gain(ffn1_pre_g, i), wt(ffn1_w_gu, i), wt(ffn1_w_down, i), gain(ffn1_post_g, i))
        sbq, sbk, sbv, dfq, dfk, dfv, gates = _proj(h, gain(mix_pre_g, i), wt(w_in, i), cos, sin)
        o_sb = _sb_attention(sbq, sbk, sbv, b, s)
        lam_init = 0.8 - 0.6 * math.exp(-0.3 * i)
        o_df = _diff_attention(diff_lambda[i].astype(F32), gain(diff_subln_g, i), dfq, dfk, dfv, b, s, lam_init)
        h = _merge(h, o_sb, o_df, gates, wt(w_branch_sb, i), wt(w_branch_diff, i), wt(w_out, i),
                   gain(mix_post_g, i))
        ple = (p[i].reshape(n, -1), gain(ple_pre_g, i), wt(w_ple_gate, i), wt(w_ple_proj, i),
               gain(ple_post_g, i))
        h = _ffn(h, gain(ffn2_pre_g, i), wt(ffn2_w_gu, i), wt(ffn2_w_down, i), gain(ffn2_post_g, i), ple=ple)
    return h.reshape(b, s, d)
```

```python
import functools
import math

import jax
import jax.numpy as jnp
from jax import lax
from jax.experimental import pallas as pl
from jax.experimental.pallas import tpu as pltpu

NORM_EPS = 1e-6
SUBLN_EPS = 1e-5
ROPE_THETA = 10000.0
SB_HEADS = 8
SB_HEAD_DIM = 64
DF_HEADS = 4
DF_QK_DIM = 64
DF_V_DIM = 2 * DF_QK_DIM
SB_W = SB_HEADS * SB_HEAD_DIM
DF_W = DF_HEADS * DF_V_DIM

LANES = 128
LOG2E = math.log2(math.e)
SIGN_BIT = -2 ** 31
EXP2_UNDERFLOW = 160.0
NO_BLOCK = 1e6
FFN_ROWS = 1024
ATTN_BLOCK = 256
SB_GROUPS_PER_STEP = 4
DF_GROUPS_PER_STEP = 4
VMEM_LIMIT_BYTES = 48 * 1024 * 1024

F32 = jnp.float32
BF16 = jnp.bfloat16


def _pick_tile(n, target):
    t = min(n, target)
    while n % t:
        t //= 2
    return t


def _rms(x, g, eps):
    return x * lax.rsqrt(jnp.mean(x * x, axis=-1, keepdims=True) + eps) * g


def _dot(a, b):
    return jnp.dot(a, b, preferred_element_type=F32)


def _dot_nt(a, b):
    return lax.dot_general(a, b, (((1,), (1,)), ((), ())), preferred_element_type=F32)


def _params(*sem):
    return pltpu.CompilerParams(dimension_semantics=sem, vmem_limit_bytes=VMEM_LIMIT_BYTES)


def _ffn_kernel(*refs, tf, with_ple):
    if with_ple:
        (h_ref, pre_g_ref, wgu_ref, wd_ref, post_g_ref,
         p_ref, ple_pre_g_ref, w_gate_ref, w_proj_ref, ple_post_g_ref, out_ref, acc_ref) = refs
    else:
        h_ref, pre_g_ref, wgu_ref, wd_ref, post_g_ref, out_ref, acc_ref = refs
    d_ff = wd_ref.shape[0]
    xn = _rms(h_ref[...], pre_g_ref[...], NORM_EPS).astype(BF16)
    for j in range(d_ff // tf):
        g = _dot(xn, wgu_ref[:, j * tf:(j + 1) * tf])
        u = _dot(xn, wgu_ref[:, d_ff + j * tf:d_ff + (j + 1) * tf])
        a = (g * jax.nn.sigmoid(g) * u).astype(BF16)
        f = _dot(a, wd_ref[j * tf:(j + 1) * tf, :])
        if j == 0:
            acc_ref[...] = f
        else:
            acc_ref[...] += f
    h = h_ref[...] + 0.5 * _rms(acc_ref[...], post_g_ref[...], NORM_EPS)
    if with_ple:
        hn = _rms(h, ple_pre_g_ref[...], NORM_EPS).astype(BF16)
        gate = jax.nn.sigmoid(_dot(hn, w_gate_ref[...]))
        e = _dot(p_ref[...].astype(BF16), w_proj_ref[...]) * gate
        h = h + _rms(e, ple_post_g_ref[...], NORM_EPS)
    out_ref[...] = h


def _resident(shape):
    return pl.BlockSpec(shape, lambda i: (0,) * len(shape), pipeline_mode=pl.Buffered(1))


def _layer_weight(w, layer, single_buffer=False):
    mode = dict(pipeline_mode=pl.Buffered(1)) if single_buffer else {}
    return pl.BlockSpec((None,) + w.shape[1:], lambda i: (layer, 0, 0), **mode)


def _ffn(h, pre_g, w_gu, w_down, post_g, layer, ple=None):
    n, d = h.shape
    d_ff = w_down.shape[1]
    tm = _pick_tile(n, FFN_ROWS)
    tf = 256 if d_ff % 256 == 0 else LANES
    row = lambda i: (i, 0)
    in_specs = [
        pl.BlockSpec((tm, d), row),
        _resident((1, d)),
        _layer_weight(w_gu, layer, single_buffer=True),
        _layer_weight(w_down, layer, single_buffer=True),
        _resident((1, d)),
    ]
    args = [h, pre_g, w_gu, w_down, post_g]
    if ple is not None:
        p, ple_pre_g, w_gate, w_proj, ple_post_g = ple
        in_specs += [
            pl.BlockSpec((tm, p.shape[1]), row),
            _resident((1, d)),
            _layer_weight(w_gate, layer, single_buffer=True),
            _layer_weight(w_proj, layer, single_buffer=True),
            _resident((1, d)),
        ]
        args += [p, ple_pre_g, w_gate, w_proj, ple_post_g]
    return pl.pallas_call(
        functools.partial(_ffn_kernel, tf=tf, with_ple=ple is not None),
        grid=(n // tm,),
        in_specs=in_specs,
        out_specs=pl.BlockSpec((tm, d), row),
        out_shape=jax.ShapeDtypeStruct((n, d), F32),
        scratch_shapes=[pltpu.VMEM((tm, d), F32)],
        compiler_params=_params("parallel"),
        name="ffn_ple" if ple is not None else "ffn",
    )(*args)


def _rope(y, cos, sin_signed):
    w = y.shape[1]
    reps = w // LANES
    cos = jnp.concatenate([cos] * reps, axis=1)
    sin_signed = jnp.concatenate([sin_signed] * reps, axis=1)
    lane = lax.broadcasted_iota(jnp.int32, y.shape, 1)
    first_half = (lane % DF_QK_DIM) < (DF_QK_DIM // 2)
    half = DF_QK_DIM // 2
    partner = jnp.where(first_half, pltpu.roll(y, w - half, 1), pltpu.roll(y, half, 1))
    return y * cos + partner * sin_signed


def _proj_kernel(h_ref, g_ref, w_ref, cos_ref, sin_ref,
                 sbq_ref, sbk_ref, sbv_ref, dfq_ref, dfk_ref, dfv_ref, gate_ref):
    u = _rms(h_ref[...], g_ref[...], NORM_EPS).astype(BF16)
    cw = SB_W
    sb_scale = LOG2E / math.sqrt(SB_HEAD_DIM)
    df_scale = LOG2E / math.sqrt(DF_QK_DIM)

    def cols(c):
        return _dot(u, w_ref[:, c * cw:(c + 1) * cw])

    sbq_ref[...] = (cols(0) * sb_scale).astype(BF16)
    sbk_ref[...] = cols(1).astype(BF16)
    sbv_ref[...] = cols(2).astype(BF16)
    cos = cos_ref[...]
    sin = sin_ref[...]
    dfq_ref[...] = (_rope(cols(3), cos, sin) * df_scale).astype(BF16)
    dfk_ref[...] = _rope(cols(4), cos, sin).astype(BF16)
    dfv_ref[...] = cols(5).astype(BF16)
    n_gate = gate_ref.shape[1] // cw
    for c in range(n_gate):
        gate_ref[:, c * cw:(c + 1) * cw] = jax.nn.sigmoid(cols(6 + c)).astype(BF16)


def _proj(h, g, w_in, layer, cos, sin):
    n, d = h.shape
    d_in = w_in.shape[2]
    tm = _pick_tile(n, 512)
    row = lambda i: (i, 0)
    const = lambda i: (0, 0)
    n_gate_cols = d_in - 6 * SB_W
    head_out = jax.ShapeDtypeStruct((n, SB_W), BF16)
    return pl.pallas_call(
        _proj_kernel,
        grid=(n // tm,),
        in_specs=[
            pl.BlockSpec((tm, d), row),
            pl.BlockSpec((1, d), const),
            _layer_weight(w_in, layer),
            pl.BlockSpec((tm, LANES), row),
            pl.BlockSpec((tm, LANES), row),
        ],
        out_specs=[pl.BlockSpec((tm, SB_W), row)] * 6 + [pl.BlockSpec((tm, n_gate_cols), row)],
        out_shape=[head_out] * 6 + [jax.ShapeDtypeStruct((n, n_gate_cols), BF16)],
        compiler_params=_params("parallel"),
        name="proj_in",
    )(h, g, w_in, cos, sin)


def _stack_halves(q):
    lane = lax.broadcasted_iota(jnp.int32, q.shape, 1)
    lo = lane < (LANES // 2)
    zero = jnp.zeros_like(q)
    return jnp.concatenate([jnp.where(lo, q, zero), jnp.where(lo, zero, q)], axis=0)


def _group(ref, g, rows=None):
    cols = slice(g * LANES, (g + 1) * LANES)
    return ref[:, cols] if rows is None else ref[rows, cols]


def _sb_kernel(q_ref, k_ref, v_ref, o_ref, acc_ref, c_ref, s_ref, w_ref, z_ref, *, blk, groups):
    qi = pl.program_id(2)
    gs = range(groups)
    q2 = [_stack_halves(_group(q_ref, g)) for g in gs]
    r = lax.broadcasted_iota(jnp.int32, (2 * blk, blk), 0) % blk
    c = lax.broadcasted_iota(jnp.int32, (2 * blk, blk), 1)
    strict = c < r
    tri2 = (r >= c).astype(BF16)

    def rows_of(j):
        return pl.ds(pl.multiple_of(j * blk, blk), blk)

    def score(g, j):
        return _dot_nt(q2[g], _group(k_ref, g, rows_of(j)))

    def suffix_sum(z, masked):
        neg_abs = lax.bitcast_convert_type(lax.bitcast_convert_type(z, jnp.int32) | SIGN_BIT, F32)
        sp = jnp.log(1.0 + jnp.exp2(neg_abs)) * LOG2E
        n = jnp.maximum(z, 0.0) + sp
        if masked:
            n = jnp.where(strict, n, 0.0)
        hi = n.astype(BF16)
        lo = (n - hi.astype(F32)).astype(BF16)
        return _dot(jnp.concatenate([hi, lo], axis=1), tri2)

    def store_weights(slot, g, z, suffix, masked):
        wg = jnp.exp2(z - suffix)
        if masked:
            wg = jnp.where(strict, wg, 0.0)
        w_ref[slot, g] = wg.astype(BF16)
        carry = c_ref[g]
        s_ref[slot, g] = jnp.exp2(-carry)
        c_ref[g] = carry + suffix[:, :1]

    def accumulate(slot, g, j):
        acc_ref[g] += s_ref[slot, g] * _dot(w_ref[slot, g], _group(v_ref, g, rows_of(j)))

    def step(cur, j):
        z, suffix = [], []
        for g in gs:
            z.append(z_ref[cur, g])
            suffix.append(suffix_sum(z[g], False))
            z_ref[1 - cur, g] = score(g, jnp.maximum(j - 1, 0))
            accumulate(cur, g, j + 1)
            if g > 0:
                store_weights(1 - cur, g - 1, z[g - 1], suffix[g - 1], False)
        store_weights(1 - cur, groups - 1, z[-1], suffix[-1], False)

    acc_ref[...] = jnp.zeros_like(acc_ref)
    c_ref[...] = jnp.zeros_like(c_ref)
    no_block = jnp.where(qi == 0, NO_BLOCK, 0.0)
    j1 = jnp.maximum(qi - 1, 0)
    z0 = [score(g, qi) for g in gs]
    z1 = [score(g, j1) for g in gs]
    for g in gs:
        suffix0 = suffix_sum(z0[g], True)
        suffix1 = suffix_sum(z1[g], False)
        store_weights(0, g, z0[g], suffix0, True)
        c_ref[g] += no_block
        accumulate(0, g, qi)
        store_weights(1, g, z1[g], suffix1, False)
        z_ref[1, g] = score(g, jnp.maximum(qi - 2, 0))

    def more(state):
        t, carry_min = state
        return jnp.logical_and(t < qi, carry_min < EXP2_UNDERFLOW)

    def body(state):
        t, _ = state
        step(t % 2, qi - 1 - t)
        return t + 1, jnp.min(c_ref[...])

    t_end, _ = lax.while_loop(more, body, (jnp.int32(1), jnp.min(c_ref[...])))
    for g in gs:
        accumulate(t_end % 2, g, jnp.maximum(qi - t_end, 0))

    lane = lax.broadcasted_iota(jnp.int32, (blk, LANES), 1)
    for g in range(groups):
        acc = acc_ref[g]
        o_ref[:, g * LANES:(g + 1) * LANES] = jnp.where(lane < LANES // 2, acc[:blk], acc[blk:]).astype(o_ref.dtype)


def _diff_kernel(lp_ref, g_ref, q_ref, k_ref, v_ref, o_ref, acc_ref, m_ref, l_ref, a_ref, p_ref, z_ref,
                 *, blk, groups, lam_init):
    qi = pl.program_id(2)
    gs = range(groups)
    q2 = [_stack_halves(_group(q_ref, g)) for g in gs]
    key = lax.broadcasted_iota(jnp.int32, (blk, 2 * blk), 0)
    qry = lax.broadcasted_iota(jnp.int32, (blk, 2 * blk), 1) % blk
    causal = key <= qry

    def rows_of(j):
        return pl.ds(pl.multiple_of(j * blk, blk), blk)

    def score(g, j):
        return _dot_nt(_group(k_ref, g, rows_of(j)), q2[g])

    def store_probs(slot, g, z, first):
        if first:
            z = jnp.where(causal, z, -jnp.inf)
        zmax = jnp.max(z, axis=0, keepdims=True)
        if first:
            m_new = zmax
        else:
            m_prev = m_ref[g]
            m_new = jnp.maximum(m_prev, zmax)
        p = jnp.exp2(z - m_new)
        psum = jnp.sum(p, axis=0, keepdims=True)
        p_ref[slot, g] = p.astype(BF16)
        if first:
            l_ref[g] = psum
            a_ref[slot, g] = jnp.ones((1, 2 * blk), F32)
        else:
            alpha = jnp.exp2(m_prev - m_new)
            l_ref[g] = alpha * l_ref[g] + psum
            a_ref[slot, g] = alpha
        m_ref[g] = m_new

    def accumulate(slot, g, j):
        pv = lax.dot_general(_group(v_ref, g, rows_of(j)), p_ref[slot, g], (((0,), (0,)), ((), ())),
                             preferred_element_type=F32)
        acc_ref[g] = a_ref[slot, g] * acc_ref[g] + pv

    def step(cur, j):
        for g in gs:
            store_probs(1 - cur, g, z_ref[cur, g], False)
            z_ref[1 - cur, g] = score(g, jnp.maximum(j - 1, 0))
            accumulate(cur, g, j + 1)

    acc_ref[...] = jnp.zeros_like(acc_ref)
    for g in gs:
        store_probs(0, g, score(g, qi), True)
        z_ref[0, g] = score(g, jnp.maximum(qi - 1, 0))

    def pair(t, _):
        j = qi - 1 - 2 * t
        step(0, j)
        step(1, j - 1)
        return 0

    lax.fori_loop(0, qi // 2, pair, 0)

    @pl.when(qi % 2 == 1)
    def _():
        step(0, 0)
        for g in gs:
            accumulate(1, g, 0)

    @pl.when(qi % 2 == 0)
    def _():
        for g in gs:
            accumulate(0, g, 0)

    lp = lp_ref[...]
    lam = (jnp.exp(jnp.sum(lp[0:1] * lp[1:2], axis=-1, keepdims=True))
           - jnp.exp(jnp.sum(lp[2:3] * lp[3:4], axis=-1, keepdims=True)) + lam_init)
    for g in range(groups):
        ot = acc_ref[g] / l_ref[g]
        o = (ot[:, :blk] - lam * ot[:, blk:]).T
        o_ref[:, g * LANES:(g + 1) * LANES] = (
            _rms(o, g_ref[...], SUBLN_EPS) * (1.0 - lam_init)).astype(o_ref.dtype)


def _attn_specs(s, blk, groups):
    w = groups * LANES
    q_spec = pl.BlockSpec((blk, w), lambda b, h, i: (b * (s // blk) + i, h))
    kv_spec = pl.BlockSpec((s, w), lambda b, h, i: (b, h))
    return q_spec, kv_spec


def _sb_attention(q, k, v, batch, s):
    n, w = q.shape
    blk = _pick_tile(s, ATTN_BLOCK)
    groups = SB_GROUPS_PER_STEP
    q_spec, kv_spec = _attn_specs(s, blk, groups)
    stat = pltpu.VMEM((groups, 2 * blk, LANES), F32)
    tile = (2, groups, 2 * blk, blk)
    return pl.pallas_call(
        functools.partial(_sb_kernel, blk=blk, groups=groups),
        grid=(batch, w // (groups * LANES), s // blk),
        in_specs=[q_spec, kv_spec, kv_spec],
        out_specs=q_spec,
        out_shape=jax.ShapeDtypeStruct((n, w), BF16),
        scratch_shapes=[stat, stat, pltpu.VMEM((2, groups, 2 * blk, LANES), F32),
                        pltpu.VMEM(tile, BF16), pltpu.VMEM(tile, F32)],
        compiler_params=_params("parallel", "parallel", "arbitrary"),
        name="sb_attn",
    )(q, k, v)


def _diff_attention(lp, g, q, k, v, batch, s, lam_init):
    n, w = q.shape
    blk = _pick_tile(s, ATTN_BLOCK)
    groups = DF_GROUPS_PER_STEP
    q_spec, kv_spec = _attn_specs(s, blk, groups)
    const = lambda b, h, i: (0, 0)
    stat = pltpu.VMEM((groups, 1, 2 * blk), F32)
    tile = (2, groups, blk, 2 * blk)
    return pl.pallas_call(
        functools.partial(_diff_kernel, blk=blk, groups=groups, lam_init=lam_init),
        grid=(batch, w // (groups * LANES), s // blk),
        in_specs=[pl.BlockSpec(lp.shape, const), pl.BlockSpec(g.shape, const), q_spec, kv_spec, kv_spec],
        out_specs=q_spec,
        out_shape=jax.ShapeDtypeStruct((n, w), BF16),
        scratch_shapes=[pltpu.VMEM((groups, LANES, 2 * blk), F32), stat, stat,
                        pltpu.VMEM((2, groups, 1, 2 * blk), F32),
                        pltpu.VMEM(tile, BF16), pltpu.VMEM(tile, F32)],
        compiler_params=_params("parallel", "parallel", "arbitrary"),
        name="diff_attn",
    )(lp, g, q, k, v)


def _merge_kernel(h_ref, osb_ref, odf_ref, gate_ref, wsb_ref, wdf_ref, wout_ref, g_ref, out_ref):
    d = h_ref.shape[1]
    y_sb = _dot(osb_ref[...], wsb_ref[...])
    y_df = _dot(odf_ref[...], wdf_ref[...])
    merged = gate_ref[:, :d].astype(F32) * y_sb + gate_ref[:, d:].astype(F32) * y_df
    out = _dot(merged.astype(BF16), wout_ref[...])
    out_ref[...] = h_ref[...] + _rms(out, g_ref[...], NORM_EPS)


def _merge(h, o_sb, o_df, gates, w_sb, w_df, w_out, layer, g):
    n, d = h.shape
    tm = _pick_tile(n, 512)
    row = lambda i: (i, 0)
    const = lambda i: (0, 0)
    return pl.pallas_call(
        _merge_kernel,
        grid=(n // tm,),
        in_specs=[
            pl.BlockSpec((tm, d), row),
            pl.BlockSpec((tm, o_sb.shape[1]), row),
            pl.BlockSpec((tm, o_df.shape[1]), row),
            pl.BlockSpec((tm, gates.shape[1]), row),
            _layer_weight(w_sb, layer),
            _layer_weight(w_df, layer),
            _layer_weight(w_out, layer),
            pl.BlockSpec((1, d), const),
        ],
        out_specs=pl.BlockSpec((tm, d), row),
        out_shape=jax.ShapeDtypeStruct((n, d), F32),
        compiler_params=_params("parallel"),
        name="merge_out",
    )(h, o_sb, o_df, gates, w_sb, w_df, w_out, g)


def _rope_tables(positions):
    half = DF_QK_DIM // 2
    inv_freq = ROPE_THETA ** (-jnp.arange(half, dtype=F32) / half)
    ang = positions.astype(F32).reshape(-1, 1) * inv_freq
    cos, sin = jnp.cos(ang), jnp.sin(ang)
    reps = LANES // DF_QK_DIM
    return (jnp.tile(jnp.concatenate([cos, cos], axis=1), (1, reps)),
            jnp.tile(jnp.concatenate([-sin, sin], axis=1), (1, reps)))


def kernel(x, p, positions, ffn1_pre_g, ffn1_w_gu, ffn1_w_down, ffn1_post_g, mix_pre_g, w_in, diff_lambda, diff_subln_g, w_branch_sb, w_branch_diff, w_out, mix_post_g, ffn2_pre_g, ffn2_w_gu, ffn2_w_down, ffn2_post_g, ple_pre_g, w_ple_gate, w_ple_proj, ple_post_g):
    b, s, d = x.shape
    depth = p.shape[0]
    n = b * s
    h = x.reshape(n, d)
    cos, sin = _rope_tables(positions)
    gain = lambda g, i: g[i].reshape(1, -1).astype(F32)
    (ffn1_w_gu, ffn1_w_down, w_in, w_branch_sb, w_branch_diff, w_out, ffn2_w_gu, ffn2_w_down,
     w_ple_gate, w_ple_proj) = (w.astype(BF16) for w in (
         ffn1_w_gu, ffn1_w_down, w_in, w_branch_sb, w_branch_diff, w_out, ffn2_w_gu, ffn2_w_down,
         w_ple_gate, w_ple_proj))
    for i in range(depth):
        h = _ffn(h, gain(ffn1_pre_g, i), ffn1_w_gu, ffn1_w_down, gain(ffn1_post_g, i), i)
        sbq, sbk, sbv, dfq, dfk, dfv, gates = _proj(h, gain(mix_pre_g, i), w_in, i, cos, sin)
        o_sb = _sb_attention(sbq, sbk, sbv, b, s)
        lam_init = 0.8 - 0.6 * math.exp(-0.3 * i)
        o_df = _diff_attention(diff_lambda[i].astype(F32), gain(diff_subln_g, i), dfq, dfk, dfv, b, s, lam_init)
        h = _merge(h, o_sb, o_df, gates, w_branch_sb, w_branch_diff, w_out, i, gain(mix_post_g, i))
        ple = (p[i].reshape(n, -1), gain(ple_pre_g, i), w_ple_gate, w_ple_proj, gain(ple_post_g, i))
        h = _ffn(h, gain(ffn2_pre_g, i), ffn2_w_gu, ffn2_w_down, gain(ffn2_post_g, i), i, ple=ple)
    return h.reshape(b, s, d)
```

```python
import functools
import math

import jax
import jax.numpy as jnp
from jax import lax
from jax.experimental import pallas as pl
from jax.experimental.pallas import tpu as pltpu

NORM_EPS = 1e-6
SUBLN_EPS = 1e-5
ROPE_THETA = 10000.0
SB_HEADS = 8
SB_HEAD_DIM = 64
DF_HEADS = 4
DF_QK_DIM = 64
DF_V_DIM = 2 * DF_QK_DIM
SB_W = SB_HEADS * SB_HEAD_DIM
DF_W = DF_HEADS * DF_V_DIM

LANES = 128
LOG2E = math.log2(math.e)
SIGN_BIT = -2 ** 31
EXP2_UNDERFLOW = 160.0
NO_BLOCK = 1e6
FFN_ROWS = 1024
ATTN_BLOCK = 256
SB_GROUPS_PER_STEP = 4
DF_GROUPS_PER_STEP = 4
VMEM_LIMIT_BYTES = 48 * 1024 * 1024

F32 = jnp.float32
BF16 = jnp.bfloat16


def _pick_tile(n, target):
    t = min(n, target)
    while n % t:
        t //= 2
    return t


def _rms(x, g, eps):
    return x * lax.rsqrt(jnp.mean(x * x, axis=-1, keepdims=True) + eps) * g


def _dot(a, b):
    return jnp.dot(a, b, preferred_element_type=F32)


def _dot_nt(a, b):
    return lax.dot_general(a, b, (((1,), (1,)), ((), ())), preferred_element_type=F32)


def _params(*sem):
    return pltpu.CompilerParams(dimension_semantics=sem, vmem_limit_bytes=VMEM_LIMIT_BYTES)


def _ffn_kernel(*refs, tf, with_ple):
    if with_ple:
        (h_ref, pre_g_ref, wgu_ref, wd_ref, post_g_ref,
         p_ref, ple_pre_g_ref, w_gate_ref, w_proj_ref, ple_post_g_ref, out_ref, acc_ref) = refs
    else:
        h_ref, pre_g_ref, wgu_ref, wd_ref, post_g_ref, out_ref, acc_ref = refs
    d_ff = wd_ref.shape[0]
    xn = _rms(h_ref[...], pre_g_ref[...], NORM_EPS).astype(BF16)
    for j in range(d_ff // tf):
        g = _dot(xn, wgu_ref[:, j * tf:(j + 1) * tf])
        u = _dot(xn, wgu_ref[:, d_ff + j * tf:d_ff + (j + 1) * tf])
        a = (g * jax.nn.sigmoid(g) * u).astype(BF16)
        f = _dot(a, wd_ref[j * tf:(j + 1) * tf, :])
        if j == 0:
            acc_ref[...] = f
        else:
            acc_ref[...] += f
    h = h_ref[...] + 0.5 * _rms(acc_ref[...], post_g_ref[...], NORM_EPS)
    if with_ple:
        hn = _rms(h, ple_pre_g_ref[...], NORM_EPS).astype(BF16)
        gate = jax.nn.sigmoid(_dot(hn, w_gate_ref[...]))
        e = _dot(p_ref[...].astype(BF16), w_proj_ref[...]) * gate
        h = h + _rms(e, ple_post_g_ref[...], NORM_EPS)
    out_ref[...] = h


def _resident(shape):
    return pl.BlockSpec(shape, lambda i: (0,) * len(shape), pipeline_mode=pl.Buffered(1))


def _layer_weight(w, layer, single_buffer=False):
    mode = dict(pipeline_mode=pl.Buffered(1)) if single_buffer else {}
    return pl.BlockSpec((None,) + w.shape[1:], lambda i: (layer, 0, 0), **mode)


def _ffn(h, pre_g, w_gu, w_down, post_g, layer, ple=None):
    n, d = h.shape
    d_ff = w_down.shape[1]
    tm = _pick_tile(n, FFN_ROWS)
    tf = 256 if d_ff % 256 == 0 else LANES
    row = lambda i: (i, 0)
    in_specs = [
        pl.BlockSpec((tm, d), row),
        _resident((1, d)),
        _layer_weight(w_gu, layer, single_buffer=True),
        _layer_weight(w_down, layer, single_buffer=True),
        _resident((1, d)),
    ]
    args = [h, pre_g, w_gu, w_down, post_g]
    if ple is not None:
        p, ple_pre_g, w_gate, w_proj, ple_post_g = ple
        in_specs += [
            pl.BlockSpec((None, tm, p.shape[2]), lambda i: (layer, i, 0)),
            _resident((1, d)),
            _layer_weight(w_gate, layer, single_buffer=True),
            _layer_weight(w_proj, layer, single_buffer=True),
            _resident((1, d)),
        ]
        args += [p, ple_pre_g, w_gate, w_proj, ple_post_g]
    return pl.pallas_call(
        functools.partial(_ffn_kernel, tf=tf, with_ple=ple is not None),
        grid=(n // tm,),
        in_specs=in_specs,
        out_specs=pl.BlockSpec((tm, d), row),
        out_shape=jax.ShapeDtypeStruct((n, d), F32),
        scratch_shapes=[pltpu.VMEM((tm, d), F32)],
        compiler_params=_params("parallel"),
        name="ffn_ple" if ple is not None else "ffn",
    )(*args)


def _rope(y, cos, sin_signed):
    w = y.shape[1]
    reps = w // LANES
    cos = jnp.concatenate([cos] * reps, axis=1)
    sin_signed = jnp.concatenate([sin_signed] * reps, axis=1)
    lane = lax.broadcasted_iota(jnp.int32, y.shape, 1)
    first_half = (lane % DF_QK_DIM) < (DF_QK_DIM // 2)
    half = DF_QK_DIM // 2
    partner = jnp.where(first_half, pltpu.roll(y, w - half, 1), pltpu.roll(y, half, 1))
    return y * cos + partner * sin_signed


def _proj_kernel(h_ref, g_ref, w_ref, cos_ref, sin_ref,
                 sbq_ref, sbk_ref, sbv_ref, dfq_ref, dfk_ref, dfv_ref, gate_ref):
    u = _rms(h_ref[...], g_ref[...], NORM_EPS).astype(BF16)
    cw = SB_W
    sb_scale = LOG2E / math.sqrt(SB_HEAD_DIM)
    df_scale = LOG2E / math.sqrt(DF_QK_DIM)

    def cols(c):
        return _dot(u, w_ref[:, c * cw:(c + 1) * cw])

    sbq_ref[...] = (cols(0) * sb_scale).astype(BF16)
    sbk_ref[...] = cols(1).astype(BF16)
    sbv_ref[...] = cols(2).astype(BF16)
    cos = cos_ref[...]
    sin = sin_ref[...]
    dfq_ref[...] = (_rope(cols(3), cos, sin) * df_scale).astype(BF16)
    dfk_ref[...] = _rope(cols(4), cos, sin).astype(BF16)
    dfv_ref[...] = cols(5).astype(BF16)
    n_gate = gate_ref.shape[1] // cw
    for c in range(n_gate):
        gate_ref[:, c * cw:(c + 1) * cw] = jax.nn.sigmoid(cols(6 + c)).astype(BF16)


def _proj(h, g, w_in, layer, cos, sin):
    n, d = h.shape
    d_in = w_in.shape[2]
    tm = _pick_tile(n, 512)
    row = lambda i: (i, 0)
    const = lambda i: (0, 0)
    n_gate_cols = d_in - 6 * SB_W
    head_out = jax.ShapeDtypeStruct((n, SB_W), BF16)
    return pl.pallas_call(
        _proj_kernel,
        grid=(n // tm,),
        in_specs=[
            pl.BlockSpec((tm, d), row),
            pl.BlockSpec((1, d), const),
            _layer_weight(w_in, layer),
            pl.BlockSpec((tm, LANES), row),
            pl.BlockSpec((tm, LANES), row),
        ],
        out_specs=[pl.BlockSpec((tm, SB_W), row)] * 6 + [pl.BlockSpec((tm, n_gate_cols), row)],
        out_shape=[head_out] * 6 + [jax.ShapeDtypeStruct((n, n_gate_cols), BF16)],
        compiler_params=_params("parallel"),
        name="proj_in",
    )(h, g, w_in, cos, sin)


def _stack_halves(q):
    lane = lax.broadcasted_iota(jnp.int32, q.shape, 1)
    lo = lane < (LANES // 2)
    zero = jnp.zeros_like(q)
    return jnp.concatenate([jnp.where(lo, q, zero), jnp.where(lo, zero, q)], axis=0)


def _group(ref, g, rows=None):
    cols = slice(g * LANES, (g + 1) * LANES)
    return ref[:, cols] if rows is None else ref[rows, cols]


def _sb_kernel(q_ref, k_ref, v_ref, o_ref, acc_ref, c_ref, s_ref, w_ref, z_ref, *, blk, groups):
    qi = pl.program_id(2)
    gs = range(groups)
    q2 = [_stack_halves(_group(q_ref, g)) for g in gs]
    r = lax.broadcasted_iota(jnp.int32, (2 * blk, blk), 0) % blk
    c = lax.broadcasted_iota(jnp.int32, (2 * blk, blk), 1)
    strict = c < r
    tri2 = (r >= c).astype(BF16)

    def rows_of(j):
        return pl.ds(pl.multiple_of(j * blk, blk), blk)

    def score(g, j):
        return _dot_nt(q2[g], _group(k_ref, g, rows_of(j)))

    def suffix_sum(z, masked):
        neg_abs = lax.bitcast_convert_type(lax.bitcast_convert_type(z, jnp.int32) | SIGN_BIT, F32)
        sp = jnp.log(1.0 + jnp.exp2(neg_abs)) * LOG2E
        n = jnp.maximum(z, 0.0) + sp
        if masked:
            n = jnp.where(strict, n, 0.0)
        hi = n.astype(BF16)
        lo = (n - hi.astype(F32)).astype(BF16)
        return _dot(jnp.concatenate([hi, lo], axis=1), tri2)

    def store_weights(slot, g, z, suffix, masked):
        wg = jnp.exp2(z - suffix)
        if masked:
            wg = jnp.where(strict, wg, 0.0)
        w_ref[slot, g] = wg.astype(BF16)
        carry = c_ref[g]
        s_ref[slot, g] = jnp.exp2(-carry)
        c_ref[g] = carry + suffix[:, :1]

    def accumulate(slot, g, j):
        acc_ref[g] += s_ref[slot, g] * _dot(w_ref[slot, g], _group(v_ref, g, rows_of(j)))

    def step(g, cur, j):
        z = z_ref[cur, g]
        suffix = suffix_sum(z, False)
        z_ref[1 - cur, g] = score(g, jnp.maximum(j - 1, 0))
        accumulate(cur, g, j + 1)
        store_weights(1 - cur, g, z, suffix, False)

    acc_ref[...] = jnp.zeros_like(acc_ref)
    c_ref[...] = jnp.zeros_like(c_ref)
    no_block = jnp.where(qi == 0, NO_BLOCK, 0.0)
    j1 = jnp.maximum(qi - 1, 0)
    z0 = [score(g, qi) for g in gs]
    z1 = [score(g, j1) for g in gs]
    for g in gs:
        suffix0 = suffix_sum(z0[g], True)
        suffix1 = suffix_sum(z1[g], False)
        store_weights(0, g, z0[g], suffix0, True)
        c_ref[g] += no_block
        accumulate(0, g, qi)
        store_weights(1, g, z1[g], suffix1, False)
        z_ref[1, g] = score(g, jnp.maximum(qi - 2, 0))

    def more(state):
        t, carry_min = state
        return jnp.logical_and(t < qi, carry_min < EXP2_UNDERFLOW)

    carry_min = [jnp.min(c_ref[g]) for g in gs]
    t_end = []
    for g in gs:
        def body(state, g=g):
            t, _ = state
            step(g, t % 2, qi - 1 - t)
            return t + 1, jnp.min(c_ref[g])

        t_end.append(lax.while_loop(more, body, (jnp.int32(1), carry_min[g]))[0])
    for g in gs:
        accumulate(t_end[g] % 2, g, jnp.maximum(qi - t_end[g], 0))

    lane = lax.broadcasted_iota(jnp.int32, (blk, LANES), 1)
    for g in range(groups):
        acc = acc_ref[g]
        o_ref[:, g * LANES:(g + 1) * LANES] = jnp.where(lane < LANES // 2, acc[:blk], acc[blk:]).astype(o_ref.dtype)


def _diff_kernel(lp_ref, g_ref, q_ref, k_ref, v_ref, o_ref, acc_ref, m_ref, l_ref, a_ref, p_ref, z_ref,
                 *, blk, groups, lam_init):
    qi = pl.program_id(2)
    gs = range(groups)
    q2 = [_stack_halves(_group(q_ref, g)) for g in gs]
    key = lax.broadcasted_iota(jnp.int32, (blk, 2 * blk), 0)
    qry = lax.broadcasted_iota(jnp.int32, (blk, 2 * blk), 1) % blk
    causal = key <= qry

    def rows_of(j):
        return pl.ds(pl.multiple_of(j * blk, blk), blk)

    def score(g, j):
        return _dot_nt(_group(k_ref, g, rows_of(j)), q2[g])

    def store_probs(slot, g, z, first):
        if first:
            z = jnp.where(causal, z, -jnp.inf)
        zmax = jnp.max(z, axis=0, keepdims=True)
        if first:
            m_new = zmax
        else:
            m_prev = m_ref[g]
            m_new = jnp.maximum(m_prev, zmax)
        p = jnp.exp2(z - m_new)
        psum = jnp.sum(p, axis=0, keepdims=True)
        p_ref[slot, g] = p.astype(BF16)
        if first:
            l_ref[g] = psum
            a_ref[slot, g] = jnp.ones((1, 2 * blk), F32)
        else:
            alpha = jnp.exp2(m_prev - m_new)
            l_ref[g] = alpha * l_ref[g] + psum
            a_ref[slot, g] = alpha
        m_ref[g] = m_new

    def accumulate(slot, g, j):
        pv = lax.dot_general(_group(v_ref, g, rows_of(j)), p_ref[slot, g], (((0,), (0,)), ((), ())),
                             preferred_element_type=F32)
        acc_ref[g] = a_ref[slot, g] * acc_ref[g] + pv

    def step(cur, j):
        for g in gs:
            store_probs(1 - cur, g, z_ref[cur, g], False)
            z_ref[1 - cur, g] = score(g, jnp.maximum(j - 1, 0))
            accumulate(cur, g, j + 1)

    acc_ref[...] = jnp.zeros_like(acc_ref)
    for g in gs:
        store_probs(0, g, score(g, qi), True)
        z_ref[0, g] = score(g, jnp.maximum(qi - 1, 0))

    def pair(t, _):
        j = qi - 1 - 2 * t
        step(0, j)
        step(1, j - 1)
        return 0

    lax.fori_loop(0, qi // 2, pair, 0)

    @pl.when(qi % 2 == 1)
    def _():
        step(0, 0)
        for g in gs:
            accumulate(1, g, 0)

    @pl.when(qi % 2 == 0)
    def _():
        for g in gs:
            accumulate(0, g, 0)

    lp = lp_ref[...]
    lam = (jnp.exp(jnp.sum(lp[0:1] * lp[1:2], axis=-1, keepdims=True))
           - jnp.exp(jnp.sum(lp[2:3] * lp[3:4], axis=-1, keepdims=True)) + lam_init)
    for g in range(groups):
        ot = acc_ref[g] / l_ref[g]
        o = (ot[:, :blk] - lam * ot[:, blk:]).T
        o_ref[:, g * LANES:(g + 1) * LANES] = (
            _rms(o, g_ref[...], SUBLN_EPS) * (1.0 - lam_init)).astype(o_ref.dtype)


def _attn_specs(s, blk, groups):
    w = groups * LANES
    q_spec = pl.BlockSpec((blk, w), lambda b, h, i: (b * (s // blk) + i, h))
    kv_spec = pl.BlockSpec((s, w), lambda b, h, i: (b, h))
    return q_spec, kv_spec


def _sb_attention(q, k, v, batch, s):
    n, w = q.shape
    blk = _pick_tile(s, ATTN_BLOCK)
    groups = SB_GROUPS_PER_STEP
    q_spec, kv_spec = _attn_specs(s, blk, groups)
    stat = pltpu.VMEM((groups, 2 * blk, LANES), F32)
    tile = (2, groups, 2 * blk, blk)
    return pl.pallas_call(
        functools.partial(_sb_kernel, blk=blk, groups=groups),
        grid=(batch, w // (groups * LANES), s // blk),
        in_specs=[q_spec, kv_spec, kv_spec],
        out_specs=q_spec,
        out_shape=jax.ShapeDtypeStruct((n, w), BF16),
        scratch_shapes=[stat, stat, pltpu.VMEM((2, groups, 2 * blk, LANES), F32),
                        pltpu.VMEM(tile, BF16), pltpu.VMEM(tile, F32)],
        compiler_params=_params("parallel", "parallel", "arbitrary"),
        name="sb_attn",
    )(q, k, v)


def _diff_attention(lp, g, q, k, v, batch, s, lam_init):
    n, w = q.shape
    blk = _pick_tile(s, ATTN_BLOCK)
    groups = DF_GROUPS_PER_STEP
    q_spec, kv_spec = _attn_specs(s, blk, groups)
    const = lambda b, h, i: (0, 0)
    stat = pltpu.VMEM((groups, 1, 2 * blk), F32)
    tile = (2, groups, blk, 2 * blk)
    return pl.pallas_call(
        functools.partial(_diff_kernel, blk=blk, groups=groups, lam_init=lam_init),
        grid=(batch, w // (groups * LANES), s // blk),
        in_specs=[pl.BlockSpec(lp.shape, const), pl.BlockSpec(g.shape, const), q_spec, kv_spec, kv_spec],
        out_specs=q_spec,
        out_shape=jax.ShapeDtypeStruct((n, w), BF16),
        scratch_shapes=[pltpu.VMEM((groups, LANES, 2 * blk), F32), stat, stat,
                        pltpu.VMEM((2, groups, 1, 2 * blk), F32),
                        pltpu.VMEM(tile, BF16), pltpu.VMEM(tile, F32)],
        compiler_params=_params("parallel", "parallel", "arbitrary"),
        name="diff_attn",
    )(lp, g, q, k, v)


def _merge_kernel(h_ref, osb_ref, odf_ref, gate_ref, wsb_ref, wdf_ref, wout_ref, g_ref, out_ref):
    d = h_ref.shape[1]
    y_sb = _dot(osb_ref[...], wsb_ref[...])
    y_df = _dot(odf_ref[...], wdf_ref[...])
    merged = gate_ref[:, :d].astype(F32) * y_sb + gate_ref[:, d:].astype(F32) * y_df
    out = _dot(merged.astype(BF16), wout_ref[...])
    out_ref[...] = h_ref[...] + _rms(out, g_ref[...], NORM_EPS)


def _merge(h, o_sb, o_df, gates, w_sb, w_df, w_out, layer, g):
    n, d = h.shape
    tm = _pick_tile(n, 512)
    row = lambda i: (i, 0)
    const = lambda i: (0, 0)
    return pl.pallas_call(
        _merge_kernel,
        grid=(n // tm,),
        in_specs=[
            pl.BlockSpec((tm, d), row),
            pl.BlockSpec((tm, o_sb.shape[1]), row),
            pl.BlockSpec((tm, o_df.shape[1]), row),
            pl.BlockSpec((tm, gates.shape[1]), row),
            _layer_weight(w_sb, layer),
            _layer_weight(w_df, layer),
            _layer_weight(w_out, layer),
            pl.BlockSpec((1, d), const),
        ],
        out_specs=pl.BlockSpec((tm, d), row),
        out_shape=jax.ShapeDtypeStruct((n, d), F32),
        compiler_params=_params("parallel"),
        name="merge_out",
    )(h, o_sb, o_df, gates, w_sb, w_df, w_out, g)


def _rope_tables(positions):
    half = DF_QK_DIM // 2
    inv_freq = ROPE_THETA ** (-jnp.arange(half, dtype=F32) / half)
    ang = positions.astype(F32).reshape(-1, 1) * inv_freq
    cos, sin = jnp.cos(ang), jnp.sin(ang)
    reps = LANES // DF_QK_DIM
    return (jnp.tile(jnp.concatenate([cos, cos], axis=1), (1, reps)),
            jnp.tile(jnp.concatenate([-sin, sin], axis=1), (1, reps)))


def kernel(x, p, positions, ffn1_pre_g, ffn1_w_gu, ffn1_w_down, ffn1_post_g, mix_pre_g, w_in, diff_lambda, diff_subln_g, w_branch_sb, w_branch_diff, w_out, mix_post_g, ffn2_pre_g, ffn2_w_gu, ffn2_w_down, ffn2_post_g, ple_pre_g, w_ple_gate, w_ple_proj, ple_post_g):
    b, s, d = x.shape
    depth = p.shape[0]
    n = b * s
    h = x.reshape(n, d)
    p_rows = p.reshape(depth, n, -1)
    cos, sin = _rope_tables(positions)
    gain = lambda g, i: g[i].reshape(1, -1).astype(F32)
    (ffn1_w_gu, ffn1_w_down, w_in, w_branch_sb, w_branch_diff, w_out, ffn2_w_gu, ffn2_w_down,
     w_ple_gate, w_ple_proj) = (w.astype(BF16) for w in (
         ffn1_w_gu, ffn1_w_down, w_in, w_branch_sb, w_branch_diff, w_out, ffn2_w_gu, ffn2_w_down,
         w_ple_gate, w_ple_proj))
    for i in range(depth):
        h = _ffn(h, gain(ffn1_pre_g, i), ffn1_w_gu, ffn1_w_down, gain(ffn1_post_g, i), i)
        sbq, sbk, sbv, dfq, dfk, dfv, gates = _proj(h, gain(mix_pre_g, i), w_in, i, cos, sin)
        o_sb = _sb_attention(sbq, sbk, sbv, b, s)
        lam_init = 0.8 - 0.6 * math.exp(-0.3 * i)
        o_df = _diff_attention(diff_lambda[i].astype(F32), gain(diff_subln_g, i), dfq, dfk, dfv, b, s, lam_init)
        h = _merge(h, o_sb, o_df, gates, w_branch_sb, w_branch_diff, w_out, i, gain(mix_post_g, i))
        ple = (p_rows, gain(ple_pre_g, i), w_ple_gate, w_ple_proj, gain(ple_post_g, i))
        h = _ffn(h, gain(ffn2_pre_g, i), ffn2_w_gu, ffn2_w_down, gain(ffn2_post_g, i), i, ple=ple)
    return h.reshape(b, s, d)
```

```python
import functools
import math

import jax
import jax.numpy as jnp
from jax import lax
from jax.experimental import pallas as pl
from jax.experimental.pallas import tpu as pltpu

NORM_EPS = 1e-6
SUBLN_EPS = 1e-5
ROPE_THETA = 10000.0
SB_HEADS = 8
SB_HEAD_DIM = 64
DF_HEADS = 4
DF_QK_DIM = 64
DF_V_DIM = 2 * DF_QK_DIM
SB_W = SB_HEADS * SB_HEAD_DIM
DF_W = DF_HEADS * DF_V_DIM
assert DF_W == SB_W == DF_HEADS * 2 * DF_QK_DIM

LANES = 128
LOG2E = math.log2(math.e)
SIGN_BIT = -2 ** 31
EXP2_UNDERFLOW = 160.0
NO_BLOCK = 1e6
MXU_WIDTH = 256
FFN_ROWS = 1024
FFN_COLS = MXU_WIDTH
PROJ_ROWS = 512
ATTN_BLOCK = MXU_WIDTH
SB_GROUPS_PER_STEP = 4
DF_GROUPS_PER_STEP = 4
VMEM_LIMIT_BYTES = 48 * 1024 * 1024

F32 = jnp.float32
BF16 = jnp.bfloat16


def _pick_tile(n, target):
    t = min(n, target)
    while n % t:
        t //= 2
    return t


def _rms(x, g, eps):
    return x * lax.rsqrt(jnp.mean(x * x, axis=-1, keepdims=True) + eps) * g


def _dot(a, b):
    return jnp.dot(a, b, preferred_element_type=F32)


def _dot_nt(a, b):
    return lax.dot_general(a, b, (((1,), (1,)), ((), ())), preferred_element_type=F32)


def _params(*sem):
    return pltpu.CompilerParams(dimension_semantics=sem, vmem_limit_bytes=VMEM_LIMIT_BYTES)


def _ffn_kernel(*refs, tf, with_ple):
    if with_ple:
        (h_ref, pre_g_ref, wgu_ref, wd_ref, post_g_ref,
         p_ref, ple_pre_g_ref, w_gate_ref, w_proj_ref, ple_post_g_ref, out_ref, acc_ref) = refs
    else:
        h_ref, pre_g_ref, wgu_ref, wd_ref, post_g_ref, out_ref, acc_ref = refs
    d_ff = wd_ref.shape[0]
    xn = _rms(h_ref[...], pre_g_ref[...], NORM_EPS).astype(BF16)
    for j in range(d_ff // tf):
        g = _dot(xn, wgu_ref[:, j * tf:(j + 1) * tf])
        u = _dot(xn, wgu_ref[:, d_ff + j * tf:d_ff + (j + 1) * tf])
        a = (g * jax.nn.sigmoid(g) * u).astype(BF16)
        f = _dot(a, wd_ref[j * tf:(j + 1) * tf, :])
        if j == 0:
            acc_ref[...] = f
        else:
            acc_ref[...] += f
    h = h_ref[...] + 0.5 * _rms(acc_ref[...], post_g_ref[...], NORM_EPS)
    if with_ple:
        hn = _rms(h, ple_pre_g_ref[...], NORM_EPS).astype(BF16)
        gate = jax.nn.sigmoid(_dot(hn, w_gate_ref[...]))
        e = _dot(p_ref[...].astype(BF16), w_proj_ref[...]) * gate
        h = h + _rms(e, ple_post_g_ref[...], NORM_EPS)
    out_ref[...] = h


def _resident(shape):
    return pl.BlockSpec(shape, lambda i: (0,) * len(shape), pipeline_mode=pl.Buffered(1))


def _layer_weight(w, layer, single_buffer=False):
    mode = dict(pipeline_mode=pl.Buffered(1)) if single_buffer else {}
    return pl.BlockSpec((None,) + w.shape[1:], lambda i: (layer, 0, 0), **mode)


def _ffn(h, pre_g, w_gu, w_down, post_g, layer, ple=None):
    n, d = h.shape
    d_ff = w_down.shape[1]
    tm = _pick_tile(n, FFN_ROWS)
    tf = FFN_COLS if d_ff % FFN_COLS == 0 else LANES
    row = lambda i: (i, 0)
    in_specs = [
        pl.BlockSpec((tm, d), row),
        _resident((1, d)),
        _layer_weight(w_gu, layer, single_buffer=True),
        _layer_weight(w_down, layer, single_buffer=True),
        _resident((1, d)),
    ]
    args = [h, pre_g, w_gu, w_down, post_g]
    if ple is not None:
        p, ple_pre_g, w_gate, w_proj, ple_post_g = ple
        in_specs += [
            pl.BlockSpec((None, tm, p.shape[2]), lambda i: (layer, i, 0)),
            _resident((1, d)),
            _layer_weight(w_gate, layer, single_buffer=True),
            _layer_weight(w_proj, layer, single_buffer=True),
            _resident((1, d)),
        ]
        args += [p, ple_pre_g, w_gate, w_proj, ple_post_g]
    return pl.pallas_call(
        functools.partial(_ffn_kernel, tf=tf, with_ple=ple is not None),
        grid=(n // tm,),
        in_specs=in_specs,
        out_specs=pl.BlockSpec((tm, d), row),
        out_shape=jax.ShapeDtypeStruct((n, d), F32),
        scratch_shapes=[pltpu.VMEM((tm, d), F32)],
        compiler_params=_params("parallel"),
        name="ffn_ple" if ple is not None else "ffn",
    )(*args)


def _rope(y, cos, sin_signed):
    w = y.shape[1]
    reps = w // LANES
    cos = jnp.concatenate([cos] * reps, axis=1)
    sin_signed = jnp.concatenate([sin_signed] * reps, axis=1)
    lane = lax.broadcasted_iota(jnp.int32, y.shape, 1)
    first_half = (lane % DF_QK_DIM) < (DF_QK_DIM // 2)
    half = DF_QK_DIM // 2
    partner = jnp.where(first_half, pltpu.roll(y, w - half, 1), pltpu.roll(y, half, 1))
    return y * cos + partner * sin_signed


def _proj_kernel(h_ref, g_ref, w_ref, cos_ref, sin_ref,
                 sbq_ref, sbk_ref, sbv_ref, dfq_ref, dfk_ref, dfv_ref, gate_ref):
    u = _rms(h_ref[...], g_ref[...], NORM_EPS).astype(BF16)
    cw = SB_W
    sb_scale = LOG2E / math.sqrt(SB_HEAD_DIM)
    df_scale = LOG2E / math.sqrt(DF_QK_DIM)

    def cols(c):
        return _dot(u, w_ref[:, c * cw:(c + 1) * cw])

    sbq_ref[...] = (cols(0) * sb_scale).astype(BF16)
    sbk_ref[...] = cols(1).astype(BF16)
    sbv_ref[...] = cols(2).astype(BF16)
    cos = cos_ref[...]
    sin = sin_ref[...]
    dfq_ref[...] = (_rope(cols(3), cos, sin) * df_scale).astype(BF16)
    dfk_ref[...] = _rope(cols(4), cos, sin).astype(BF16)
    dfv_ref[...] = cols(5).astype(BF16)
    n_gate = gate_ref.shape[1] // cw
    for c in range(n_gate):
        gate_ref[:, c * cw:(c + 1) * cw] = jax.nn.sigmoid(cols(6 + c)).astype(BF16)


def _proj(h, g, w_in, layer, cos, sin):
    n, d = h.shape
    d_in = w_in.shape[2]
    tm = _pick_tile(n, PROJ_ROWS)
    row = lambda i: (i, 0)
    const = lambda i: (0, 0)
    n_gate_cols = d_in - 6 * SB_W
    head_out = jax.ShapeDtypeStruct((n, SB_W), BF16)
    return pl.pallas_call(
        _proj_kernel,
        grid=(n // tm,),
        in_specs=[
            pl.BlockSpec((tm, d), row),
            pl.BlockSpec((1, d), const),
            _layer_weight(w_in, layer),
            pl.BlockSpec((tm, LANES), row),
            pl.BlockSpec((tm, LANES), row),
        ],
        out_specs=[pl.BlockSpec((tm, SB_W), row)] * 6 + [pl.BlockSpec((tm, n_gate_cols), row)],
        out_shape=[head_out] * 6 + [jax.ShapeDtypeStruct((n, n_gate_cols), BF16)],
        compiler_params=_params("parallel"),
        name="proj_in",
    )(h, g, w_in, cos, sin)


def _stack_halves(q):
    lane = lax.broadcasted_iota(jnp.int32, q.shape, 1)
    lo = lane < (LANES // 2)
    zero = jnp.zeros_like(q)
    return jnp.concatenate([jnp.where(lo, q, zero), jnp.where(lo, zero, q)], axis=0)


def _group(ref, g, rows=None):
    cols = slice(g * LANES, (g + 1) * LANES)
    return ref[:, cols] if rows is None else ref[rows, cols]


def _sb_kernel(q_ref, k_ref, v_ref, o_ref, acc_ref, c_ref, s_ref, w_ref, z_ref, *, blk, groups):
    qi = pl.program_id(2)
    gs = range(groups)
    q2 = [_stack_halves(_group(q_ref, g)) for g in gs]
    r = lax.broadcasted_iota(jnp.int32, (2 * blk, blk), 0) % blk
    c = lax.broadcasted_iota(jnp.int32, (2 * blk, blk), 1)
    strict = c < r
    tri2 = (r >= c).astype(BF16)

    def rows_of(j):
        return pl.ds(pl.multiple_of(j * blk, blk), blk)

    def score(g, j):
        return _dot_nt(q2[g], _group(k_ref, g, rows_of(j)))

    def suffix_sum(z, masked):
        neg_abs = lax.bitcast_convert_type(lax.bitcast_convert_type(z, jnp.int32) | SIGN_BIT, F32)
        sp = jnp.log(1.0 + jnp.exp2(neg_abs)) * LOG2E
        n = jnp.maximum(z, 0.0) + sp
        if masked:
            n = jnp.where(strict, n, 0.0)
        hi = n.astype(BF16)
        lo = (n - hi.astype(F32)).astype(BF16)
        return _dot(jnp.concatenate([hi, lo], axis=1), tri2)

    def store_weights(slot, g, z, suffix, masked):
        wg = jnp.exp2(z - suffix)
        if masked:
            wg = jnp.where(strict, wg, 0.0)
        w_ref[slot, g] = wg.astype(BF16)
        carry = c_ref[g]
        s_ref[slot, g] = jnp.exp2(-carry)
        c_ref[g] = carry + suffix[:, :1]

    def accumulate(slot, g, j):
        acc_ref[g] += s_ref[slot, g] * _dot(w_ref[slot, g], _group(v_ref, g, rows_of(j)))

    def step(g, cur, j):
        z = z_ref[cur, g]
        suffix = suffix_sum(z, False)
        z_ref[1 - cur, g] = score(g, jnp.maximum(j - 1, 0))
        accumulate(cur, g, j + 1)
        store_weights(1 - cur, g, z, suffix, False)

    acc_ref[...] = jnp.zeros_like(acc_ref)
    c_ref[...] = jnp.zeros_like(c_ref)
    no_block = jnp.where(qi == 0, NO_BLOCK, 0.0)
    j1 = jnp.maximum(qi - 1, 0)
    z0 = [score(g, qi) for g in gs]
    z1 = [score(g, j1) for g in gs]
    for g in gs:
        suffix0 = suffix_sum(z0[g], True)
        suffix1 = suffix_sum(z1[g], False)
        store_weights(0, g, z0[g], suffix0, True)
        c_ref[g] += no_block
        accumulate(0, g, qi)
        store_weights(1, g, z1[g], suffix1, False)
        z_ref[1, g] = score(g, jnp.maximum(qi - 2, 0))

    def more(state):
        t, carry_min = state
        return jnp.logical_and(t < qi, carry_min < EXP2_UNDERFLOW)

    carry_min = [jnp.min(c_ref[g]) for g in gs]
    t_end = []
    for g in gs:
        def body(state, g=g):
            t, _ = state
            step(g, t % 2, qi - 1 - t)
            return t + 1, jnp.min(c_ref[g])

        t_end.append(lax.while_loop(more, body, (jnp.int32(1), carry_min[g]))[0])
    for g in gs:
        accumulate(t_end[g] % 2, g, jnp.maximum(qi - t_end[g], 0))

    lane = lax.broadcasted_iota(jnp.int32, (blk, LANES), 1)
    for g in range(groups):
        acc = acc_ref[g]
        o_ref[:, g * LANES:(g + 1) * LANES] = jnp.where(lane < LANES // 2, acc[:blk], acc[blk:]).astype(o_ref.dtype)


def _diff_kernel(lp_ref, g_ref, q_ref, k_ref, v_ref, o_ref, acc_ref, m_ref, l_ref, a_ref, p_ref, z_ref,
                 *, blk, groups, lam_init):
    qi = pl.program_id(2)
    gs = range(groups)
    q2 = [_stack_halves(_group(q_ref, g)) for g in gs]
    key = lax.broadcasted_iota(jnp.int32, (blk, 2 * blk), 0)
    qry = lax.broadcasted_iota(jnp.int32, (blk, 2 * blk), 1) % blk
    causal = key <= qry

    def rows_of(j):
        return pl.ds(pl.multiple_of(j * blk, blk), blk)

    def score(g, j):
        return _dot_nt(_group(k_ref, g, rows_of(j)), q2[g])

    def store_probs(slot, g, z, first):
        if first:
            z = jnp.where(causal, z, -jnp.inf)
        zmax = jnp.max(z, axis=0, keepdims=True)
        if first:
            m_new = zmax
        else:
            m_prev = m_ref[g]
            m_new = jnp.maximum(m_prev, zmax)
        p = jnp.exp2(z - m_new)
        psum = jnp.sum(p, axis=0, keepdims=True)
        p_ref[slot, g] = p.astype(BF16)
        if first:
            l_ref[g] = psum
            a_ref[slot, g] = jnp.ones((1, 2 * blk), F32)
        else:
            alpha = jnp.exp2(m_prev - m_new)
            l_ref[g] = alpha * l_ref[g] + psum
            a_ref[slot, g] = alpha
        m_ref[g] = m_new

    def accumulate(slot, g, j):
        pv = lax.dot_general(_group(v_ref, g, rows_of(j)), p_ref[slot, g], (((0,), (0,)), ((), ())),
                             preferred_element_type=F32)
        acc_ref[g] = a_ref[slot, g] * acc_ref[g] + pv

    def step(cur, j):
        for g in gs:
            store_probs(1 - cur, g, z_ref[cur, g], False)
            z_ref[1 - cur, g] = score(g, jnp.maximum(j - 1, 0))
            accumulate(cur, g, j + 1)

    acc_ref[...] = jnp.zeros_like(acc_ref)
    for g in gs:
        store_probs(0, g, score(g, qi), True)
        z_ref[0, g] = score(g, jnp.maximum(qi - 1, 0))

    def pair(t, _):
        j = qi - 1 - 2 * t
        step(0, j)
        step(1, j - 1)
        return 0

    lax.fori_loop(0, qi // 2, pair, 0)

    @pl.when(qi % 2 == 1)
    def _():
        step(0, 0)
        for g in gs:
            accumulate(1, g, 0)

    @pl.when(qi % 2 == 0)
    def _():
        for g in gs:
            accumulate(0, g, 0)

    lp = lp_ref[...]
    lam = (jnp.exp(jnp.sum(lp[0:1] * lp[1:2], axis=-1, keepdims=True))
           - jnp.exp(jnp.sum(lp[2:3] * lp[3:4], axis=-1, keepdims=True)) + lam_init)
    for g in range(groups):
        ot = acc_ref[g] / l_ref[g]
        o = (ot[:, :blk] - lam * ot[:, blk:]).T
        o_ref[:, g * LANES:(g + 1) * LANES] = (
            _rms(o, g_ref[...], SUBLN_EPS) * (1.0 - lam_init)).astype(o_ref.dtype)


def _attn_specs(s, blk, groups):
    w = groups * LANES
    q_spec = pl.BlockSpec((blk, w), lambda b, h, i: (b * (s // blk) + i, h))
    kv_spec = pl.BlockSpec((s, w), lambda b, h, i: (b, h))
    return q_spec, kv_spec


def _sb_attention(q, k, v, batch, s):
    n, w = q.shape
    blk = _pick_tile(s, ATTN_BLOCK)
    groups = SB_GROUPS_PER_STEP
    q_spec, kv_spec = _attn_specs(s, blk, groups)
    stat = pltpu.VMEM((groups, 2 * blk, LANES), F32)
    tile = (2, groups, 2 * blk, blk)
    return pl.pallas_call(
        functools.partial(_sb_kernel, blk=blk, groups=groups),
        grid=(batch, w // (groups * LANES), s // blk),
        in_specs=[q_spec, kv_spec, kv_spec],
        out_specs=q_spec,
        out_shape=jax.ShapeDtypeStruct((n, w), BF16),
        scratch_shapes=[stat, stat, pltpu.VMEM((2, groups, 2 * blk, LANES), F32),
                        pltpu.VMEM(tile, BF16), pltpu.VMEM(tile, F32)],
        compiler_params=_params("parallel", "parallel", "arbitrary"),
        name="sb_attn",
    )(q, k, v)


def _diff_attention(lp, g, q, k, v, batch, s, lam_init):
    n, w = q.shape
    blk = _pick_tile(s, ATTN_BLOCK)
    groups = DF_GROUPS_PER_STEP
    q_spec, kv_spec = _attn_specs(s, blk, groups)
    const = lambda b, h, i: (0, 0)
    stat = pltpu.VMEM((groups, 1, 2 * blk), F32)
    tile = (2, groups, blk, 2 * blk)
    return pl.pallas_call(
        functools.partial(_diff_kernel, blk=blk, groups=groups, lam_init=lam_init),
        grid=(batch, w // (groups * LANES), s // blk),
        in_specs=[pl.BlockSpec(lp.shape, const), pl.BlockSpec(g.shape, const), q_spec, kv_spec, kv_spec],
        out_specs=q_spec,
        out_shape=jax.ShapeDtypeStruct((n, w), BF16),
        scratch_shapes=[pltpu.VMEM((groups, LANES, 2 * blk), F32), stat, stat,
                        pltpu.VMEM((2, groups, 1, 2 * blk), F32),
                        pltpu.VMEM(tile, BF16), pltpu.VMEM(tile, F32)],
        compiler_params=_params("parallel", "parallel", "arbitrary"),
        name="diff_attn",
    )(lp, g, q, k, v)


def _merge_kernel(h_ref, osb_ref, odf_ref, gate_ref, wsb_ref, wdf_ref, wout_ref, g_ref, out_ref):
    d = h_ref.shape[1]
    y_sb = _dot(osb_ref[...], wsb_ref[...])
    y_df = _dot(odf_ref[...], wdf_ref[...])
    merged = gate_ref[:, :d].astype(F32) * y_sb + gate_ref[:, d:].astype(F32) * y_df
    out = _dot(merged.astype(BF16), wout_ref[...])
    out_ref[...] = h_ref[...] + _rms(out, g_ref[...], NORM_EPS)


def _merge(h, o_sb, o_df, gates, w_sb, w_df, w_out, layer, g):
    n, d = h.shape
    tm = _pick_tile(n, PROJ_ROWS)
    row = lambda i: (i, 0)
    const = lambda i: (0, 0)
    return pl.pallas_call(
        _merge_kernel,
        grid=(n // tm,),
        in_specs=[
            pl.BlockSpec((tm, d), row),
            pl.BlockSpec((tm, o_sb.shape[1]), row),
            pl.BlockSpec((tm, o_df.shape[1]), row),
            pl.BlockSpec((tm, gates.shape[1]), row),
            _layer_weight(w_sb, layer),
            _layer_weight(w_df, layer),
            _layer_weight(w_out, layer),
            pl.BlockSpec((1, d), const),
        ],
        out_specs=pl.BlockSpec((tm, d), row),
        out_shape=jax.ShapeDtypeStruct((n, d), F32),
        compiler_params=_params("parallel"),
        name="merge_out",
    )(h, o_sb, o_df, gates, w_sb, w_df, w_out, g)


def _rope_tables(positions):
    half = DF_QK_DIM // 2
    inv_freq = ROPE_THETA ** (-jnp.arange(half, dtype=F32) / half)
    ang = positions.astype(F32).reshape(-1, 1) * inv_freq
    cos, sin = jnp.cos(ang), jnp.sin(ang)
    reps = LANES // DF_QK_DIM
    return (jnp.tile(jnp.concatenate([cos, cos], axis=1), (1, reps)),
            jnp.tile(jnp.concatenate([-sin, sin], axis=1), (1, reps)))


def kernel(x, p, positions, ffn1_pre_g, ffn1_w_gu, ffn1_w_down, ffn1_post_g, mix_pre_g, w_in, diff_lambda, diff_subln_g, w_branch_sb, w_branch_diff, w_out, mix_post_g, ffn2_pre_g, ffn2_w_gu, ffn2_w_down, ffn2_post_g, ple_pre_g, w_ple_gate, w_ple_proj, ple_post_g):
    b, s, d = x.shape
    depth = p.shape[0]
    n = b * s
    h = x.reshape(n, d)
    p_rows = p.reshape(depth, n, -1)
    cos, sin = _rope_tables(positions)
    gain = lambda g, i: g[i].reshape(1, -1).astype(F32)
    (ffn1_w_gu, ffn1_w_down, w_in, w_branch_sb, w_branch_diff, w_out, ffn2_w_gu, ffn2_w_down,
     w_ple_gate, w_ple_proj) = (w.astype(BF16) for w in (
         ffn1_w_gu, ffn1_w_down, w_in, w_branch_sb, w_branch_diff, w_out, ffn2_w_gu, ffn2_w_down,
         w_ple_gate, w_ple_proj))
    for i in range(depth):
        h = _ffn(h, gain(ffn1_pre_g, i), ffn1_w_gu, ffn1_w_down, gain(ffn1_post_g, i), i)
        sbq, sbk, sbv, dfq, dfk, dfv, gates = _proj(h, gain(mix_pre_g, i), w_in, i, cos, sin)
        o_sb = _sb_attention(sbq, sbk, sbv, b, s)
        lam_init = 0.8 - 0.6 * math.exp(-0.3 * i)
        o_df = _diff_attention(diff_lambda[i].astype(F32), gain(diff_subln_g, i), dfq, dfk, dfv, b, s, lam_init)
        h = _merge(h, o_sb, o_df, gates, w_branch_sb, w_branch_diff, w_out, i, gain(mix_post_g, i))
        ple = (p_rows, gain(ple_pre_g, i), w_ple_gate, w_ple_proj, gain(ple_post_g, i))
        h = _ffn(h, gain(ffn2_pre_g, i), ffn2_w_gu, ffn2_w_down, gain(ffn2_post_g, i), i, ple=ple)
    return h.reshape(b, s, d)
```

```python
import functools
import math

import jax
import jax.numpy as jnp
from jax import lax
from jax.experimental import pallas as pl
from jax.experimental.pallas import tpu as pltpu

NORM_EPS = 1e-6
SUBLN_EPS = 1e-5
ROPE_THETA = 10000.0
SB_HEADS = 8
SB_HEAD_DIM = 64
DF_HEADS = 4
DF_QK_DIM = 64
DF_V_DIM = 2 * DF_QK_DIM
SB_W = SB_HEADS * SB_HEAD_DIM
DF_W = DF_HEADS * DF_V_DIM
assert DF_W == SB_W == DF_HEADS * 2 * DF_QK_DIM

LANES = 128
LOG2E = math.log2(math.e)
EXP2_UNDERFLOW = 160.0
NO_BLOCK = 1e6
MXU_WIDTH = 256
FFN_ROWS = 1024
FFN_COLS = MXU_WIDTH
PROJ_ROWS = 512
ATTN_BLOCK = MXU_WIDTH
SB_GROUPS_PER_STEP = 4
DF_GROUPS_PER_STEP = 4
VMEM_LIMIT_BYTES = 48 * 1024 * 1024

F32 = jnp.float32
BF16 = jnp.bfloat16


def _pick_tile(n, target):
    t = min(n, target)
    while n % t:
        t //= 2
    return t


def _rms(x, g, eps):
    return x * lax.rsqrt(jnp.mean(x * x, axis=-1, keepdims=True) + eps) * g


def _dot(a, b):
    return jnp.dot(a, b, preferred_element_type=F32)


def _dot_nt(a, b):
    return lax.dot_general(a, b, (((1,), (1,)), ((), ())), preferred_element_type=F32)


def _params(*sem):
    return pltpu.CompilerParams(dimension_semantics=sem, vmem_limit_bytes=VMEM_LIMIT_BYTES)


def _ffn_kernel(*refs, tf, with_ple):
    if with_ple:
        (h_ref, pre_g_ref, wgu_ref, wd_ref, post_g_ref,
         p_ref, ple_pre_g_ref, w_gate_ref, w_proj_ref, ple_post_g_ref, out_ref, acc_ref) = refs
    else:
        h_ref, pre_g_ref, wgu_ref, wd_ref, post_g_ref, out_ref, acc_ref = refs
    d_ff = wd_ref.shape[0]
    xn = _rms(h_ref[...], pre_g_ref[...], NORM_EPS).astype(BF16)
    for j in range(d_ff // tf):
        g = _dot(xn, wgu_ref[:, j * tf:(j + 1) * tf])
        u = _dot(xn, wgu_ref[:, d_ff + j * tf:d_ff + (j + 1) * tf])
        a = (g * jax.nn.sigmoid(g) * u).astype(BF16)
        f = _dot(a, wd_ref[j * tf:(j + 1) * tf, :])
        if j == 0:
            acc_ref[...] = f
        else:
            acc_ref[...] += f
    h = h_ref[...] + 0.5 * _rms(acc_ref[...], post_g_ref[...], NORM_EPS)
    if with_ple:
        hn = _rms(h, ple_pre_g_ref[...], NORM_EPS).astype(BF16)
        gate = jax.nn.sigmoid(_dot(hn, w_gate_ref[...]))
        e = _dot(p_ref[...].astype(BF16), w_proj_ref[...]) * gate
        h = h + _rms(e, ple_post_g_ref[...], NORM_EPS)
    out_ref[...] = h


def _resident(shape):
    return pl.BlockSpec(shape, lambda i: (0,) * len(shape), pipeline_mode=pl.Buffered(1))


def _layer_weight(w, layer, single_buffer=False):
    mode = dict(pipeline_mode=pl.Buffered(1)) if single_buffer else {}
    return pl.BlockSpec((None,) + w.shape[1:], lambda i: (layer, 0, 0), **mode)


def _ffn(h, pre_g, w_gu, w_down, post_g, layer, ple=None):
    n, d = h.shape
    d_ff = w_down.shape[1]
    tm = _pick_tile(n, FFN_ROWS)
    tf = FFN_COLS if d_ff % FFN_COLS == 0 else LANES
    row = lambda i: (i, 0)
    in_specs = [
        pl.BlockSpec((tm, d), row),
        _resident((1, d)),
        _layer_weight(w_gu, layer, single_buffer=True),
        _layer_weight(w_down, layer, single_buffer=True),
        _resident((1, d)),
    ]
    args = [h, pre_g, w_gu, w_down, post_g]
    if ple is not None:
        p, ple_pre_g, w_gate, w_proj, ple_post_g = ple
        in_specs += [
            pl.BlockSpec((None, tm, p.shape[2]), lambda i: (layer, i, 0)),
            _resident((1, d)),
            _layer_weight(w_gate, layer, single_buffer=True),
            _layer_weight(w_proj, layer, single_buffer=True),
            _resident((1, d)),
        ]
        args += [p, ple_pre_g, w_gate, w_proj, ple_post_g]
    return pl.pallas_call(
        functools.partial(_ffn_kernel, tf=tf, with_ple=ple is not None),
        grid=(n // tm,),
        in_specs=in_specs,
        out_specs=pl.BlockSpec((tm, d), row),
        out_shape=jax.ShapeDtypeStruct((n, d), F32),
        scratch_shapes=[pltpu.VMEM((tm, d), F32)],
        compiler_params=_params("parallel"),
        name="ffn_ple" if ple is not None else "ffn",
    )(*args)


def _rope(y, cos, sin_signed):
    w = y.shape[1]
    reps = w // LANES
    cos = jnp.concatenate([cos] * reps, axis=1)
    sin_signed = jnp.concatenate([sin_signed] * reps, axis=1)
    lane = lax.broadcasted_iota(jnp.int32, y.shape, 1)
    first_half = (lane % DF_QK_DIM) < (DF_QK_DIM // 2)
    half = DF_QK_DIM // 2
    partner = jnp.where(first_half, pltpu.roll(y, w - half, 1), pltpu.roll(y, half, 1))
    return y * cos + partner * sin_signed


def _proj_kernel(h_ref, g_ref, w_ref, cos_ref, sin_ref,
                 sbq_ref, sbk_ref, sbv_ref, dfq_ref, dfk_ref, dfv_ref, gate_ref):
    u = _rms(h_ref[...], g_ref[...], NORM_EPS).astype(BF16)
    cw = SB_W
    sb_scale = LOG2E / math.sqrt(SB_HEAD_DIM)
    df_scale = LOG2E / math.sqrt(DF_QK_DIM)

    def cols(c):
        return _dot(u, w_ref[:, c * cw:(c + 1) * cw])

    sbq_ref[...] = (cols(0) * sb_scale).astype(BF16)
    sbk_ref[...] = cols(1).astype(BF16)
    sbv_ref[...] = cols(2).astype(BF16)
    cos = cos_ref[...]
    sin = sin_ref[...]
    dfq_ref[...] = (_rope(cols(3), cos, sin) * df_scale).astype(BF16)
    dfk_ref[...] = _rope(cols(4), cos, sin).astype(BF16)
    dfv_ref[...] = cols(5).astype(BF16)
    n_gate = gate_ref.shape[1] // cw
    for c in range(n_gate):
        gate_ref[:, c * cw:(c + 1) * cw] = jax.nn.sigmoid(cols(6 + c)).astype(BF16)


def _proj(h, g, w_in, layer, cos, sin):
    n, d = h.shape
    d_in = w_in.shape[2]
    tm = _pick_tile(n, PROJ_ROWS)
    row = lambda i: (i, 0)
    const = lambda i: (0, 0)
    n_gate_cols = d_in - 6 * SB_W
    head_out = jax.ShapeDtypeStruct((n, SB_W), BF16)
    return pl.pallas_call(
        _proj_kernel,
        grid=(n // tm,),
        in_specs=[
            pl.BlockSpec((tm, d), row),
            pl.BlockSpec((1, d), const),
            _layer_weight(w_in, layer),
            pl.BlockSpec((tm, LANES), row),
            pl.BlockSpec((tm, LANES), row),
        ],
        out_specs=[pl.BlockSpec((tm, SB_W), row)] * 6 + [pl.BlockSpec((tm, n_gate_cols), row)],
        out_shape=[head_out] * 6 + [jax.ShapeDtypeStruct((n, n_gate_cols), BF16)],
        compiler_params=_params("parallel"),
        name="proj_in",
    )(h, g, w_in, cos, sin)


def _stack_halves(q):
    lane = lax.broadcasted_iota(jnp.int32, q.shape, 1)
    lo = lane < (LANES // 2)
    zero = jnp.zeros_like(q)
    return jnp.concatenate([jnp.where(lo, q, zero), jnp.where(lo, zero, q)], axis=0)


def _group(ref, g, rows=None):
    cols = slice(g * LANES, (g + 1) * LANES)
    return ref[:, cols] if rows is None else ref[rows, cols]


def _sb_kernel(q_ref, k_ref, v_ref, o_ref, acc_ref, c_ref, s_ref, w_ref, z_ref, *, blk, groups):
    qi = pl.program_id(2)
    gs = range(groups)
    q2 = [_stack_halves(_group(q_ref, g)) for g in gs]
    r = lax.broadcasted_iota(jnp.int32, (2 * blk, blk), 0) % blk
    c = lax.broadcasted_iota(jnp.int32, (2 * blk, blk), 1)
    strict = c < r
    tri2 = (r >= c).astype(BF16)

    def rows_of(j):
        return pl.ds(pl.multiple_of(j * blk, blk), blk)

    def score(g, j):
        return _dot_nt(q2[g], _group(k_ref, g, rows_of(j)))

    def suffix_sum(z, masked):
        sp = jnp.log(1.0 + jnp.exp2(-jnp.abs(z))) * LOG2E
        n = jnp.maximum(z, 0.0) + sp
        if masked:
            n = jnp.where(strict, n, 0.0)
        hi = n.astype(BF16)
        lo = (n - hi.astype(F32)).astype(BF16)
        return _dot(jnp.concatenate([hi, lo], axis=1), tri2)

    def store_weights(slot, g, z, suffix, masked):
        wg = jnp.exp2(z - suffix)
        if masked:
            wg = jnp.where(strict, wg, 0.0)
        w_ref[slot, g] = wg.astype(BF16)
        carry = c_ref[g]
        s_ref[slot, g] = jnp.exp2(-carry)
        c_ref[g] = carry + suffix[:, :1]

    def accumulate(slot, g, j):
        acc_ref[g] += s_ref[slot, g] * _dot(w_ref[slot, g], _group(v_ref, g, rows_of(j)))

    def step(g, cur, j):
        z = z_ref[cur, g]
        suffix = suffix_sum(z, False)
        z_ref[1 - cur, g] = score(g, jnp.maximum(j - 1, 0))
        accumulate(cur, g, j + 1)
        store_weights(1 - cur, g, z, suffix, False)

    acc_ref[...] = jnp.zeros_like(acc_ref)
    c_ref[...] = jnp.zeros_like(c_ref)
    no_block = jnp.where(qi == 0, NO_BLOCK, 0.0)
    j1 = jnp.maximum(qi - 1, 0)
    z0 = [score(g, qi) for g in gs]
    z1 = [score(g, j1) for g in gs]
    for g in gs:
        suffix0 = suffix_sum(z0[g], True)
        suffix1 = suffix_sum(z1[g], False)
        store_weights(0, g, z0[g], suffix0, True)
        c_ref[g] += no_block
        accumulate(0, g, qi)
        store_weights(1, g, z1[g], suffix1, False)
        z_ref[1, g] = score(g, jnp.maximum(qi - 2, 0))

    def more(state):
        t, carry_min = state
        return jnp.logical_and(t < qi, carry_min < EXP2_UNDERFLOW)

    carry_min = [jnp.min(c_ref[g]) for g in gs]
    t_end = []
    for g in gs:
        def body(state, g=g):
            t, _ = state
            step(g, t % 2, qi - 1 - t)
            return t + 1, jnp.min(c_ref[g])

        t_end.append(lax.while_loop(more, body, (jnp.int32(1), carry_min[g]))[0])
    for g in gs:
        accumulate(t_end[g] % 2, g, jnp.maximum(qi - t_end[g], 0))

    lane = lax.broadcasted_iota(jnp.int32, (blk, LANES), 1)
    for g in range(groups):
        acc = acc_ref[g]
        o_ref[:, g * LANES:(g + 1) * LANES] = jnp.where(lane < LANES // 2, acc[:blk], acc[blk:]).astype(o_ref.dtype)


def _diff_kernel(lp_ref, g_ref, q_ref, k_ref, v_ref, o_ref, acc_ref, m_ref, l_ref, a_ref, p_ref, z_ref,
                 *, blk, groups, lam_init):
    qi = pl.program_id(2)
    gs = range(groups)
    q2 = [_stack_halves(_group(q_ref, g)) for g in gs]
    key = lax.broadcasted_iota(jnp.int32, (blk, 2 * blk), 0)
    qry = lax.broadcasted_iota(jnp.int32, (blk, 2 * blk), 1) % blk
    causal = key <= qry

    def rows_of(j):
        return pl.ds(pl.multiple_of(j * blk, blk), blk)

    def score(g, j):
        return _dot_nt(_group(k_ref, g, rows_of(j)), q2[g])

    def store_probs(slot, g, z, first):
        if first:
            z = jnp.where(causal, z, -jnp.inf)
        zmax = jnp.max(z, axis=0, keepdims=True)
        if first:
            m_new = zmax
        else:
            m_prev = m_ref[g]
            m_new = jnp.maximum(m_prev, zmax)
        p = jnp.exp2(z - m_new)
        psum = jnp.sum(p, axis=0, keepdims=True)
        p_ref[slot, g] = p.astype(BF16)
        if first:
            l_ref[g] = psum
            a_ref[slot, g] = jnp.ones((1, 2 * blk), F32)
        else:
            alpha = jnp.exp2(m_prev - m_new)
            l_ref[g] = alpha * l_ref[g] + psum
            a_ref[slot, g] = alpha
        m_ref[g] = m_new

    def accumulate(slot, g, j):
        pv = lax.dot_general(_group(v_ref, g, rows_of(j)), p_ref[slot, g], (((0,), (0,)), ((), ())),
                             preferred_element_type=F32)
        acc_ref[g] = a_ref[slot, g] * acc_ref[g] + pv

    def step(cur, j):
        for g in gs:
            store_probs(1 - cur, g, z_ref[cur, g], False)
            z_ref[1 - cur, g] = score(g, jnp.maximum(j - 1, 0))
            accumulate(cur, g, j + 1)

    acc_ref[...] = jnp.zeros_like(acc_ref)
    for g in gs:
        store_probs(0, g, score(g, qi), True)
        z_ref[0, g] = score(g, jnp.maximum(qi - 1, 0))

    def pair(t, _):
        j = qi - 1 - 2 * t
        step(0, j)
        step(1, j - 1)
        return 0

    lax.fori_loop(0, qi // 2, pair, 0)

    @pl.when(qi % 2 == 1)
    def _():
        step(0, 0)
        for g in gs:
            accumulate(1, g, 0)

    @pl.when(qi % 2 == 0)
    def _():
        for g in gs:
            accumulate(0, g, 0)

    lp = lp_ref[...]
    lam = (jnp.exp(jnp.sum(lp[0:1] * lp[1:2], axis=-1, keepdims=True))
           - jnp.exp(jnp.sum(lp[2:3] * lp[3:4], axis=-1, keepdims=True)) + lam_init)
    for g in range(groups):
        ot = acc_ref[g] / l_ref[g]
        o = (ot[:, :blk] - lam * ot[:, blk:]).T
        o_ref[:, g * LANES:(g + 1) * LANES] = (
            _rms(o, g_ref[...], SUBLN_EPS) * (1.0 - lam_init)).astype(o_ref.dtype)


def _attn_specs(s, blk, groups):
    w = groups * LANES
    q_spec = pl.BlockSpec((blk, w), lambda b, h, i: (b * (s // blk) + i, h))
    kv_spec = pl.BlockSpec((s, w), lambda b, h, i: (b, h))
    return q_spec, kv_spec


def _sb_attention(q, k, v, batch, s):
    n, w = q.shape
    blk = _pick_tile(s, ATTN_BLOCK)
    groups = SB_GROUPS_PER_STEP
    q_spec, kv_spec = _attn_specs(s, blk, groups)
    stat = pltpu.VMEM((groups, 2 * blk, LANES), F32)
    tile = (2, groups, 2 * blk, blk)
    return pl.pallas_call(
        functools.partial(_sb_kernel, blk=blk, groups=groups),
        grid=(batch, w // (groups * LANES), s // blk),
        in_specs=[q_spec, kv_spec, kv_spec],
        out_specs=q_spec,
        out_shape=jax.ShapeDtypeStruct((n, w), BF16),
        scratch_shapes=[stat, stat, pltpu.VMEM((2, groups, 2 * blk, LANES), F32),
                        pltpu.VMEM(tile, BF16), pltpu.VMEM(tile, F32)],
        compiler_params=_params("parallel", "parallel", "arbitrary"),
        name="sb_attn",
    )(q, k, v)


def _diff_attention(lp, g, q, k, v, batch, s, lam_init):
    n, w = q.shape
    blk = _pick_tile(s, ATTN_BLOCK)
    groups = DF_GROUPS_PER_STEP
    q_spec, kv_spec = _attn_specs(s, blk, groups)
    const = lambda b, h, i: (0, 0)
    stat = pltpu.VMEM((groups, 1, 2 * blk), F32)
    tile = (2, groups, blk, 2 * blk)
    return pl.pallas_call(
        functools.partial(_diff_kernel, blk=blk, groups=groups, lam_init=lam_init),
        grid=(batch, w // (groups * LANES), s // blk),
        in_specs=[pl.BlockSpec(lp.shape, const), pl.BlockSpec(g.shape, const), q_spec, kv_spec, kv_spec],
        out_specs=q_spec,
        out_shape=jax.ShapeDtypeStruct((n, w), BF16),
        scratch_shapes=[pltpu.VMEM((groups, LANES, 2 * blk), F32), stat, stat,
                        pltpu.VMEM((2, groups, 1, 2 * blk), F32),
                        pltpu.VMEM(tile, BF16), pltpu.VMEM(tile, F32)],
        compiler_params=_params("parallel", "parallel", "arbitrary"),
        name="diff_attn",
    )(lp, g, q, k, v)


def _merge_kernel(h_ref, osb_ref, odf_ref, gate_ref, wsb_ref, wdf_ref, wout_ref, g_ref, out_ref):
    d = h_ref.shape[1]
    y_sb = _dot(osb_ref[...], wsb_ref[...])
    y_df = _dot(odf_ref[...], wdf_ref[...])
    merged = gate_ref[:, :d].astype(F32) * y_sb + gate_ref[:, d:].astype(F32) * y_df
    out = _dot(merged.astype(BF16), wout_ref[...])
    out_ref[...] = h_ref[...] + _rms(out, g_ref[...], NORM_EPS)


def _merge(h, o_sb, o_df, gates, w_sb, w_df, w_out, layer, g):
    n, d = h.shape
    tm = _pick_tile(n, PROJ_ROWS)
    row = lambda i: (i, 0)
    const = lambda i: (0, 0)
    return pl.pallas_call(
        _merge_kernel,
        grid=(n // tm,),
        in_specs=[
            pl.BlockSpec((tm, d), row),
            pl.BlockSpec((tm, o_sb.shape[1]), row),
            pl.BlockSpec((tm, o_df.shape[1]), row),
            pl.BlockSpec((tm, gates.shape[1]), row),
            _layer_weight(w_sb, layer),
            _layer_weight(w_df, layer),
            _layer_weight(w_out, layer),
            pl.BlockSpec((1, d), const),
        ],
        out_specs=pl.BlockSpec((tm, d), row),
        out_shape=jax.ShapeDtypeStruct((n, d), F32),
        compiler_params=_params("parallel"),
        name="merge_out",
    )(h, o_sb, o_df, gates, w_sb, w_df, w_out, g)


def _rope_tables(positions):
    half = DF_QK_DIM // 2
    inv_freq = ROPE_THETA ** (-jnp.arange(half, dtype=F32) / half)
    ang = positions.astype(F32).reshape(-1, 1) * inv_freq
    cos, sin = jnp.cos(ang), jnp.sin(ang)
    reps = LANES // DF_QK_DIM
    return (jnp.tile(jnp.concatenate([cos, cos], axis=1), (1, reps)),
            jnp.tile(jnp.concatenate([-sin, sin], axis=1), (1, reps)))


def kernel(x, p, positions, ffn1_pre_g, ffn1_w_gu, ffn1_w_down, ffn1_post_g, mix_pre_g, w_in, diff_lambda, diff_subln_g, w_branch_sb, w_branch_diff, w_out, mix_post_g, ffn2_pre_g, ffn2_w_gu, ffn2_w_down, ffn2_post_g, ple_pre_g, w_ple_gate, w_ple_proj, ple_post_g):
    b, s, d = x.shape
    depth = p.shape[0]
    n = b * s
    h = x.reshape(n, d)
    p_rows = p.reshape(depth, n, -1)
    cos, sin = _rope_tables(positions)
    gain = lambda g, i: g[i].reshape(1, -1).astype(F32)
    (ffn1_w_gu, ffn1_w_down, w_in, w_branch_sb, w_branch_diff, w_out, ffn2_w_gu, ffn2_w_down,
     w_ple_gate, w_ple_proj) = (w.astype(BF16) for w in (
         ffn1_w_gu, ffn1_w_down, w_in, w_branch_sb, w_branch_diff, w_out, ffn2_w_gu, ffn2_w_down,
         w_ple_gate, w_ple_proj))
    for i in range(depth):
        h = _ffn(h, gain(ffn1_pre_g, i), ffn1_w_gu, ffn1_w_down, gain(ffn1_post_g, i), i)
        sbq, sbk, sbv, dfq, dfk, dfv, gates = _proj(h, gain(mix_pre_g, i), w_in, i, cos, sin)
        o_sb = _sb_attention(sbq, sbk, sbv, b, s)
        lam_init = 0.8 - 0.6 * math.exp(-0.3 * i)
        o_df = _diff_attention(diff_lambda[i].astype(F32), gain(diff_subln_g, i), dfq, dfk, dfv, b, s, lam_init)
        h = _merge(h, o_sb, o_df, gates, w_branch_sb, w_branch_diff, w_out, i, gain(mix_post_g, i))
        ple = (p_rows, gain(ple_pre_g, i), w_ple_gate, w_ple_proj, gain(ple_post_g, i))
        h = _ffn(h, gain(ffn2_pre_g, i), ffn2_w_gu, ffn2_w_down, gain(ffn2_post_g, i), i, ple=ple)
    return h.reshape(b, s, d)
```

```python
import functools
import math

import jax
import jax.numpy as jnp
from jax import lax
from jax.experimental import pallas as pl
from jax.experimental.pallas import tpu as pltpu

NORM_EPS = 1e-6
SUBLN_EPS = 1e-5
ROPE_THETA = 10000.0
SB_HEADS = 8
SB_HEAD_DIM = 64
DF_HEADS = 4
DF_QK_DIM = 64
DF_V_DIM = 2 * DF_QK_DIM
SB_W = SB_HEADS * SB_HEAD_DIM
DF_W = DF_HEADS * DF_V_DIM
assert DF_W == SB_W == DF_HEADS * 2 * DF_QK_DIM

LANES = 128
LOG2E = math.log2(math.e)
EXP2_UNDERFLOW = 160.0
NO_BLOCK = 1e6
MXU_WIDTH = 256
FFN_ROWS = 512
FFN_COLS = MXU_WIDTH
PROJ_ROWS = 512
ATTN_BLOCK = MXU_WIDTH
SB_GROUPS_PER_STEP = 4
DF_GROUPS_PER_STEP = 4
VMEM_LIMIT_BYTES = 48 * 1024 * 1024

F32 = jnp.float32
BF16 = jnp.bfloat16


def _pick_tile(n, target):
    t = min(n, target)
    while n % t:
        t //= 2
    return t


def _rms(x, g, eps):
    return x * lax.rsqrt(jnp.mean(x * x, axis=-1, keepdims=True) + eps) * g


def _dot(a, b):
    return jnp.dot(a, b, preferred_element_type=F32)


def _dot_nt(a, b):
    return lax.dot_general(a, b, (((1,), (1,)), ((), ())), preferred_element_type=F32)


def _params(*sem):
    return pltpu.CompilerParams(dimension_semantics=sem, vmem_limit_bytes=VMEM_LIMIT_BYTES)


def _ffn_kernel(*refs, tf, with_ple):
    if with_ple:
        (h_ref, pre_g_ref, wgu_ref, wd_ref, post_g_ref,
         p_ref, ple_pre_g_ref, w_gate_ref, w_proj_ref, ple_post_g_ref, out_ref, acc_ref) = refs
    else:
        h_ref, pre_g_ref, wgu_ref, wd_ref, post_g_ref, out_ref, acc_ref = refs
    d_ff = wd_ref.shape[0]
    xn = _rms(h_ref[...], pre_g_ref[...], NORM_EPS).astype(BF16)
    for j in range(d_ff // tf):
        g = _dot(xn, wgu_ref[:, j * tf:(j + 1) * tf])
        u = _dot(xn, wgu_ref[:, d_ff + j * tf:d_ff + (j + 1) * tf])
        a = (g * jax.nn.sigmoid(g) * u).astype(BF16)
        f = _dot(a, wd_ref[j * tf:(j + 1) * tf, :])
        if j == 0:
            acc_ref[...] = f
        else:
            acc_ref[...] += f
    h = h_ref[...] + 0.5 * _rms(acc_ref[...], post_g_ref[...], NORM_EPS)
    if with_ple:
        hn = _rms(h, ple_pre_g_ref[...], NORM_EPS).astype(BF16)
        gate = jax.nn.sigmoid(_dot(hn, w_gate_ref[...]))
        e = _dot(p_ref[...].astype(BF16), w_proj_ref[...]) * gate
        h = h + _rms(e, ple_post_g_ref[...], NORM_EPS)
    out_ref[...] = h


def _resident(shape):
    return pl.BlockSpec(shape, lambda i: (0,) * len(shape), pipeline_mode=pl.Buffered(1))


def _layer_weight(w, layer, single_buffer=False):
    mode = dict(pipeline_mode=pl.Buffered(1)) if single_buffer else {}
    return pl.BlockSpec((None,) + w.shape[1:], lambda i: (layer, 0, 0), **mode)


def _ffn(h, pre_g, w_gu, w_down, post_g, layer, ple=None):
    n, d = h.shape
    d_ff = w_down.shape[1]
    tm = _pick_tile(n, FFN_ROWS)
    tf = FFN_COLS if d_ff % FFN_COLS == 0 else LANES
    row = lambda i: (i, 0)
    in_specs = [
        pl.BlockSpec((tm, d), row),
        _resident((1, d)),
        _layer_weight(w_gu, layer, single_buffer=True),
        _layer_weight(w_down, layer, single_buffer=True),
        _resident((1, d)),
    ]
    args = [h, pre_g, w_gu, w_down, post_g]
    if ple is not None:
        p, ple_pre_g, w_gate, w_proj, ple_post_g = ple
        in_specs += [
            pl.BlockSpec((None, tm, p.shape[2]), lambda i: (layer, i, 0)),
            _resident((1, d)),
            _layer_weight(w_gate, layer, single_buffer=True),
            _layer_weight(w_proj, layer, single_buffer=True),
            _resident((1, d)),
        ]
        args += [p, ple_pre_g, w_gate, w_proj, ple_post_g]
    return pl.pallas_call(
        functools.partial(_ffn_kernel, tf=tf, with_ple=ple is not None),
        grid=(n // tm,),
        in_specs=in_specs,
        out_specs=pl.BlockSpec((tm, d), row),
        out_shape=jax.ShapeDtypeStruct((n, d), F32),
        scratch_shapes=[pltpu.VMEM((tm, d), F32)],
        compiler_params=_params("parallel"),
        name="ffn_ple" if ple is not None else "ffn",
    )(*args)


def _rope(y, cos, sin_signed):
    w = y.shape[1]
    reps = w // LANES
    cos = jnp.concatenate([cos] * reps, axis=1)
    sin_signed = jnp.concatenate([sin_signed] * reps, axis=1)
    lane = lax.broadcasted_iota(jnp.int32, y.shape, 1)
    first_half = (lane % DF_QK_DIM) < (DF_QK_DIM // 2)
    half = DF_QK_DIM // 2
    partner = jnp.where(first_half, pltpu.roll(y, w - half, 1), pltpu.roll(y, half, 1))
    return y * cos + partner * sin_signed


def _proj_kernel(h_ref, g_ref, w_ref, cos_ref, sin_ref,
                 sbq_ref, sbk_ref, sbv_ref, dfq_ref, dfk_ref, dfv_ref, gate_ref):
    u = _rms(h_ref[...], g_ref[...], NORM_EPS).astype(BF16)
    cw = SB_W
    sb_scale = LOG2E / math.sqrt(SB_HEAD_DIM)
    df_scale = LOG2E / math.sqrt(DF_QK_DIM)

    def cols(c):
        return _dot(u, w_ref[:, c * cw:(c + 1) * cw])

    sbq_ref[...] = (cols(0) * sb_scale).astype(BF16)
    sbk_ref[...] = cols(1).astype(BF16)
    sbv_ref[...] = cols(2).astype(BF16)
    cos = cos_ref[...]
    sin = sin_ref[...]
    dfq_ref[...] = (_rope(cols(3), cos, sin) * df_scale).astype(BF16)
    dfk_ref[...] = _rope(cols(4), cos, sin).astype(BF16)
    dfv_ref[...] = cols(5).astype(BF16)
    n_gate = gate_ref.shape[1] // cw
    for c in range(n_gate):
        gate_ref[:, c * cw:(c + 1) * cw] = jax.nn.sigmoid(cols(6 + c)).astype(BF16)


def _proj(h, g, w_in, layer, cos, sin):
    n, d = h.shape
    d_in = w_in.shape[2]
    tm = _pick_tile(n, PROJ_ROWS)
    row = lambda i: (i, 0)
    const = lambda i: (0, 0)
    n_gate_cols = d_in - 6 * SB_W
    head_out = jax.ShapeDtypeStruct((n, SB_W), BF16)
    return pl.pallas_call(
        _proj_kernel,
        grid=(n // tm,),
        in_specs=[
            pl.BlockSpec((tm, d), row),
            pl.BlockSpec((1, d), const),
            _layer_weight(w_in, layer),
            pl.BlockSpec((tm, LANES), row),
            pl.BlockSpec((tm, LANES), row),
        ],
        out_specs=[pl.BlockSpec((tm, SB_W), row)] * 6 + [pl.BlockSpec((tm, n_gate_cols), row)],
        out_shape=[head_out] * 6 + [jax.ShapeDtypeStruct((n, n_gate_cols), BF16)],
        compiler_params=_params("parallel"),
        name="proj_in",
    )(h, g, w_in, cos, sin)


def _stack_halves(q):
    lane = lax.broadcasted_iota(jnp.int32, q.shape, 1)
    lo = lane < (LANES // 2)
    zero = jnp.zeros_like(q)
    return jnp.concatenate([jnp.where(lo, q, zero), jnp.where(lo, zero, q)], axis=0)


def _group(ref, g, rows=None):
    cols = slice(g * LANES, (g + 1) * LANES)
    return ref[:, cols] if rows is None else ref[rows, cols]


def _sb_kernel(q_ref, k_ref, v_ref, o_ref, acc_ref, c_ref, s_ref, w_ref, z_ref, *, blk, groups):
    qi = pl.program_id(2)
    gs = range(groups)
    q2 = [_stack_halves(_group(q_ref, g)) for g in gs]
    r = lax.broadcasted_iota(jnp.int32, (2 * blk, blk), 0) % blk
    c = lax.broadcasted_iota(jnp.int32, (2 * blk, blk), 1)
    strict = c < r
    tri2 = (r >= c).astype(BF16)

    def rows_of(j):
        return pl.ds(pl.multiple_of(j * blk, blk), blk)

    def score(g, j):
        return _dot_nt(q2[g], _group(k_ref, g, rows_of(j)))

    def suffix_sum(z, masked):
        sp = jnp.log(1.0 + jnp.exp2(-jnp.abs(z))) * LOG2E
        n = jnp.maximum(z, 0.0) + sp
        if masked:
            n = jnp.where(strict, n, 0.0)
        hi = n.astype(BF16)
        lo = (n - hi.astype(F32)).astype(BF16)
        return _dot(jnp.concatenate([hi, lo], axis=1), tri2)

    def store_weights(slot, g, z, suffix, masked):
        wg = jnp.exp2(z - suffix)
        if masked:
            wg = jnp.where(strict, wg, 0.0)
        w_ref[slot, g] = wg.astype(BF16)
        carry = c_ref[g]
        s_ref[slot, g] = jnp.exp2(-carry)
        c_ref[g] = carry + suffix[:, :1]

    def accumulate(slot, g, j):
        acc_ref[g] += s_ref[slot, g] * _dot(w_ref[slot, g], _group(v_ref, g, rows_of(j)))

    def step(g, cur, j):
        z = z_ref[cur, g]
        suffix = suffix_sum(z, False)
        z_ref[1 - cur, g] = score(g, jnp.maximum(j - 1, 0))
        accumulate(cur, g, j + 1)
        store_weights(1 - cur, g, z, suffix, False)

    acc_ref[...] = jnp.zeros_like(acc_ref)
    c_ref[...] = jnp.zeros_like(c_ref)
    no_block = jnp.where(qi == 0, NO_BLOCK, 0.0)
    j1 = jnp.maximum(qi - 1, 0)
    z0 = [score(g, qi) for g in gs]
    z1 = [score(g, j1) for g in gs]
    for g in gs:
        suffix0 = suffix_sum(z0[g], True)
        suffix1 = suffix_sum(z1[g], False)
        store_weights(0, g, z0[g], suffix0, True)
        c_ref[g] += no_block
        accumulate(0, g, qi)
        store_weights(1, g, z1[g], suffix1, False)
        z_ref[1, g] = score(g, jnp.maximum(qi - 2, 0))

    def more(state):
        t, carry_min = state
        return jnp.logical_and(t < qi, carry_min < EXP2_UNDERFLOW)

    carry_min = [jnp.min(c_ref[g]) for g in gs]
    t_end = []
    for g in gs:
        def body(state, g=g):
            t, _ = state
            step(g, t % 2, qi - 1 - t)
            return t + 1, jnp.min(c_ref[g])

        t_end.append(lax.while_loop(more, body, (jnp.int32(1), carry_min[g]))[0])
    for g in gs:
        accumulate(t_end[g] % 2, g, jnp.maximum(qi - t_end[g], 0))

    lane = lax.broadcasted_iota(jnp.int32, (blk, LANES), 1)
    for g in range(groups):
        acc = acc_ref[g]
        o_ref[:, g * LANES:(g + 1) * LANES] = jnp.where(lane < LANES // 2, acc[:blk], acc[blk:]).astype(o_ref.dtype)


def _diff_kernel(lp_ref, g_ref, q_ref, k_ref, v_ref, o_ref, acc_ref, m_ref, l_ref, a_ref, p_ref, z_ref,
                 *, blk, groups, lam_init):
    qi = pl.program_id(2)
    gs = range(groups)
    q2 = [_stack_halves(_group(q_ref, g)) for g in gs]
    key = lax.broadcasted_iota(jnp.int32, (blk, 2 * blk), 0)
    qry = lax.broadcasted_iota(jnp.int32, (blk, 2 * blk), 1) % blk
    causal = key <= qry

    def rows_of(j):
        return pl.ds(pl.multiple_of(j * blk, blk), blk)

    def score(g, j):
        return _dot_nt(_group(k_ref, g, rows_of(j)), q2[g])

    def store_probs(slot, g, z, first):
        if first:
            z = jnp.where(causal, z, -jnp.inf)
        zmax = jnp.max(z, axis=0, keepdims=True)
        if first:
            m_new = zmax
        else:
            m_prev = m_ref[g]
            m_new = jnp.maximum(m_prev, zmax)
        p = jnp.exp2(z - m_new)
        psum = jnp.sum(p, axis=0, keepdims=True)
        p_ref[slot, g] = p.astype(BF16)
        if first:
            l_ref[g] = psum
            a_ref[slot, g] = jnp.ones((1, 2 * blk), F32)
        else:
            alpha = jnp.exp2(m_prev - m_new)
            l_ref[g] = alpha * l_ref[g] + psum
            a_ref[slot, g] = alpha
        m_ref[g] = m_new

    def accumulate(slot, g, j):
        pv = lax.dot_general(_group(v_ref, g, rows_of(j)), p_ref[slot, g], (((0,), (0,)), ((), ())),
                             preferred_element_type=F32)
        acc_ref[g] = a_ref[slot, g] * acc_ref[g] + pv

    def step(cur, j):
        for g in gs:
            store_probs(1 - cur, g, z_ref[cur, g], False)
            z_ref[1 - cur, g] = score(g, jnp.maximum(j - 1, 0))
            accumulate(cur, g, j + 1)

    acc_ref[...] = jnp.zeros_like(acc_ref)
    for g in gs:
        store_probs(0, g, score(g, qi), True)
        z_ref[0, g] = score(g, jnp.maximum(qi - 1, 0))

    def pair(t, _):
        j = qi - 1 - 2 * t
        step(0, j)
        step(1, j - 1)
        return 0

    lax.fori_loop(0, qi // 2, pair, 0)

    @pl.when(qi % 2 == 1)
    def _():
        step(0, 0)
        for g in gs:
            accumulate(1, g, 0)

    @pl.when(qi % 2 == 0)
    def _():
        for g in gs:
            accumulate(0, g, 0)

    lp = lp_ref[...]
    lam = (jnp.exp(jnp.sum(lp[0:1] * lp[1:2], axis=-1, keepdims=True))
           - jnp.exp(jnp.sum(lp[2:3] * lp[3:4], axis=-1, keepdims=True)) + lam_init)
    for g in range(groups):
        ot = acc_ref[g] / l_ref[g]
        o = (ot[:, :blk] - lam * ot[:, blk:]).T
        o_ref[:, g * LANES:(g + 1) * LANES] = (
            _rms(o, g_ref[...], SUBLN_EPS) * (1.0 - lam_init)).astype(o_ref.dtype)


def _attn_specs(s, blk, groups):
    w = groups * LANES
    q_spec = pl.BlockSpec((blk, w), lambda b, h, i: (b * (s // blk) + i, h))
    kv_spec = pl.BlockSpec((s, w), lambda b, h, i: (b, h))
    return q_spec, kv_spec


def _sb_attention(q, k, v, batch, s):
    n, w = q.shape
    blk = _pick_tile(s, ATTN_BLOCK)
    groups = SB_GROUPS_PER_STEP
    q_spec, kv_spec = _attn_specs(s, blk, groups)
    stat = pltpu.VMEM((groups, 2 * blk, LANES), F32)
    tile = (2, groups, 2 * blk, blk)
    return pl.pallas_call(
        functools.partial(_sb_kernel, blk=blk, groups=groups),
        grid=(batch, w // (groups * LANES), s // blk),
        in_specs=[q_spec, kv_spec, kv_spec],
        out_specs=q_spec,
        out_shape=jax.ShapeDtypeStruct((n, w), BF16),
        scratch_shapes=[stat, stat, pltpu.VMEM((2, groups, 2 * blk, LANES), F32),
                        pltpu.VMEM(tile, BF16), pltpu.VMEM(tile, F32)],
        compiler_params=_params("parallel", "parallel", "arbitrary"),
        name="sb_attn",
    )(q, k, v)


def _diff_attention(lp, g, q, k, v, batch, s, lam_init):
    n, w = q.shape
    blk = _pick_tile(s, ATTN_BLOCK)
    groups = DF_GROUPS_PER_STEP
    q_spec, kv_spec = _attn_specs(s, blk, groups)
    const = lambda b, h, i: (0, 0)
    stat = pltpu.VMEM((groups, 1, 2 * blk), F32)
    tile = (2, groups, blk, 2 * blk)
    return pl.pallas_call(
        functools.partial(_diff_kernel, blk=blk, groups=groups, lam_init=lam_init),
        grid=(batch, w // (groups * LANES), s // blk),
        in_specs=[pl.BlockSpec(lp.shape, const), pl.BlockSpec(g.shape, const), q_spec, kv_spec, kv_spec],
        out_specs=q_spec,
        out_shape=jax.ShapeDtypeStruct((n, w), BF16),
        scratch_shapes=[pltpu.VMEM((groups, LANES, 2 * blk), F32), stat, stat,
                        pltpu.VMEM((2, groups, 1, 2 * blk), F32),
                        pltpu.VMEM(tile, BF16), pltpu.VMEM(tile, F32)],
        compiler_params=_params("parallel", "parallel", "arbitrary"),
        name="diff_attn",
    )(lp, g, q, k, v)


def _merge_kernel(h_ref, osb_ref, odf_ref, gate_ref, wsb_ref, wdf_ref, wout_ref, g_ref, out_ref):
    d = h_ref.shape[1]
    y_sb = _dot(osb_ref[...], wsb_ref[...])
    y_df = _dot(odf_ref[...], wdf_ref[...])
    merged = gate_ref[:, :d].astype(F32) * y_sb + gate_ref[:, d:].astype(F32) * y_df
    out = _dot(merged.astype(BF16), wout_ref[...])
    out_ref[...] = h_ref[...] + _rms(out, g_ref[...], NORM_EPS)


def _merge(h, o_sb, o_df, gates, w_sb, w_df, w_out, layer, g):
    n, d = h.shape
    tm = _pick_tile(n, PROJ_ROWS)
    row = lambda i: (i, 0)
    const = lambda i: (0, 0)
    return pl.pallas_call(
        _merge_kernel,
        grid=(n // tm,),
        in_specs=[
            pl.BlockSpec((tm, d), row),
            pl.BlockSpec((tm, o_sb.shape[1]), row),
            pl.BlockSpec((tm, o_df.shape[1]), row),
            pl.BlockSpec((tm, gates.shape[1]), row),
            _layer_weight(w_sb, layer),
            _layer_weight(w_df, layer),
            _layer_weight(w_out, layer),
            pl.BlockSpec((1, d), const),
        ],
        out_specs=pl.BlockSpec((tm, d), row),
        out_shape=jax.ShapeDtypeStruct((n, d), F32),
        compiler_params=_params("parallel"),
        name="merge_out",
    )(h, o_sb, o_df, gates, w_sb, w_df, w_out, g)


def _rope_tables(positions):
    half = DF_QK_DIM // 2
    inv_freq = ROPE_THETA ** (-jnp.arange(half, dtype=F32) / half)
    ang = positions.astype(F32).reshape(-1, 1) * inv_freq
    cos, sin = jnp.cos(ang), jnp.sin(ang)
    reps = LANES // DF_QK_DIM
    return (jnp.tile(jnp.concatenate([cos, cos], axis=1), (1, reps)),
            jnp.tile(jnp.concatenate([-sin, sin], axis=1), (1, reps)))


def kernel(x, p, positions, ffn1_pre_g, ffn1_w_gu, ffn1_w_down, ffn1_post_g, mix_pre_g, w_in, diff_lambda, diff_subln_g, w_branch_sb, w_branch_diff, w_out, mix_post_g, ffn2_pre_g, ffn2_w_gu, ffn2_w_down, ffn2_post_g, ple_pre_g, w_ple_gate, w_ple_proj, ple_post_g):
    b, s, d = x.shape
    depth = p.shape[0]
    n = b * s
    h = x.reshape(n, d)
    p_rows = p.reshape(depth, n, -1)
    cos, sin = _rope_tables(positions)
    gain = lambda g, i: g[i].reshape(1, -1).astype(F32)
    (ffn1_w_gu, ffn1_w_down, w_in, w_branch_sb, w_branch_diff, w_out, ffn2_w_gu, ffn2_w_down,
     w_ple_gate, w_ple_proj) = (w.astype(BF16) for w in (
         ffn1_w_gu, ffn1_w_down, w_in, w_branch_sb, w_branch_diff, w_out, ffn2_w_gu, ffn2_w_down,
         w_ple_gate, w_ple_proj))
    for i in range(depth):
        h = _ffn(h, gain(ffn1_pre_g, i), ffn1_w_gu, ffn1_w_down, gain(ffn1_post_g, i), i)
        sbq, sbk, sbv, dfq, dfk, dfv, gates = _proj(h, gain(mix_pre_g, i), w_in, i, cos, sin)
        o_sb = _sb_attention(sbq, sbk, sbv, b, s)
        lam_init = 0.8 - 0.6 * math.exp(-0.3 * i)
        o_df = _diff_attention(diff_lambda[i].astype(F32), gain(diff_subln_g, i), dfq, dfk, dfv, b, s, lam_init)
        h = _merge(h, o_sb, o_df, gates, w_branch_sb, w_branch_diff, w_out, i, gain(mix_post_g, i))
        ple = (p_rows, gain(ple_pre_g, i), w_ple_gate, w_ple_proj, gain(ple_post_g, i))
        h = _ffn(h, gain(ffn2_pre_g, i), ffn2_w_gu, ffn2_w_down, gain(ffn2_post_g, i), i, ple=ple)
    return h.reshape(b, s, d)
```

```python
import functools
import math

import jax
import jax.numpy as jnp
from jax import lax
from jax.experimental import pallas as pl
from jax.experimental.pallas import tpu as pltpu

NORM_EPS = 1e-6
SUBLN_EPS = 1e-5
ROPE_THETA = 10000.0
SB_HEADS = 8
SB_HEAD_DIM = 64
DF_HEADS = 4
DF_QK_DIM = 64
DF_V_DIM = 2 * DF_QK_DIM
SB_W = SB_HEADS * SB_HEAD_DIM
DF_W = DF_HEADS * DF_V_DIM
assert DF_W == SB_W == DF_HEADS * 2 * DF_QK_DIM

LANES = 128
LOG2E = math.log2(math.e)
EXP2_UNDERFLOW = 160.0
NO_BLOCK = 1e6
MXU_WIDTH = 256
FFN_ROWS = 1024
FFN_PARTS = 2
FFN_COLS = MXU_WIDTH
PROJ_ROWS = 512
ATTN_BLOCK = MXU_WIDTH
SB_GROUPS_PER_STEP = 4
DF_GROUPS_PER_STEP = 4
VMEM_LIMIT_BYTES = 48 * 1024 * 1024

F32 = jnp.float32
BF16 = jnp.bfloat16


def _pick_tile(n, target):
    t = min(n, target)
    while n % t:
        t //= 2
    return t


def _rms(x, g, eps):
    return x * lax.rsqrt(jnp.mean(x * x, axis=-1, keepdims=True) + eps) * g


def _dot(a, b):
    return jnp.dot(a, b, preferred_element_type=F32)


def _dot_nt(a, b):
    return lax.dot_general(a, b, (((1,), (1,)), ((), ())), preferred_element_type=F32)


def _params(*sem):
    return pltpu.CompilerParams(dimension_semantics=sem, vmem_limit_bytes=VMEM_LIMIT_BYTES)


def _ffn_kernel(*refs, tf, with_ple):
    if with_ple:
        (h_ref, pre_g_ref, wgu_ref, wd_ref, post_g_ref,
         p_ref, ple_pre_g_ref, w_gate_ref, w_proj_ref, ple_post_g_ref, out_ref, acc_ref) = refs
    else:
        h_ref, pre_g_ref, wgu_ref, wd_ref, post_g_ref, out_ref, acc_ref = refs
    d_ff = wd_ref.shape[0]
    n_chunks = d_ff // tf
    part = h_ref.shape[0] // FFN_PARTS
    rows = [slice(r * part, (r + 1) * part) for r in range(FFN_PARTS)]
    xn = [None] * FFN_PARTS

    def prologue(r):
        xn[r] = _rms(h_ref[rows[r]], pre_g_ref[...], NORM_EPS).astype(BF16)

    def chunk(r, j):
        g = _dot(xn[r], wgu_ref[:, j * tf:(j + 1) * tf])
        u = _dot(xn[r], wgu_ref[:, d_ff + j * tf:d_ff + (j + 1) * tf])
        a = (g * jax.nn.sigmoid(g) * u).astype(BF16)
        f = _dot(a, wd_ref[j * tf:(j + 1) * tf, :])
        if j == 0:
            acc_ref[rows[r]] = f
        else:
            acc_ref[rows[r]] += f

    def epilogue(r):
        h = h_ref[rows[r]] + 0.5 * _rms(acc_ref[rows[r]], post_g_ref[...], NORM_EPS)
        if with_ple:
            hn = _rms(h, ple_pre_g_ref[...], NORM_EPS).astype(BF16)
            gate = jax.nn.sigmoid(_dot(hn, w_gate_ref[...]))
            e = _dot(p_ref[rows[r]].astype(BF16), w_proj_ref[...]) * gate
            h = h + _rms(e, ple_post_g_ref[...], NORM_EPS)
        out_ref[rows[r]] = h

    for t in range(n_chunks + FFN_PARTS - 1):
        for r in range(FFN_PARTS):
            j = t - r
            if j == 0:
                prologue(r)
            if 0 <= j < n_chunks:
                chunk(r, j)
            if j == n_chunks - 1:
                epilogue(r)


def _resident(shape):
    return pl.BlockSpec(shape, lambda i: (0,) * len(shape), pipeline_mode=pl.Buffered(1))


def _layer_weight(w, layer, single_buffer=False):
    mode = dict(pipeline_mode=pl.Buffered(1)) if single_buffer else {}
    return pl.BlockSpec((None,) + w.shape[1:], lambda i: (layer, 0, 0), **mode)


def _ffn(h, pre_g, w_gu, w_down, post_g, layer, ple=None):
    n, d = h.shape
    d_ff = w_down.shape[1]
    tm = _pick_tile(n, FFN_ROWS)
    tf = FFN_COLS if d_ff % FFN_COLS == 0 else LANES
    row = lambda i: (i, 0)
    in_specs = [
        pl.BlockSpec((tm, d), row),
        _resident((1, d)),
        _layer_weight(w_gu, layer, single_buffer=True),
        _layer_weight(w_down, layer, single_buffer=True),
        _resident((1, d)),
    ]
    args = [h, pre_g, w_gu, w_down, post_g]
    if ple is not None:
        p, ple_pre_g, w_gate, w_proj, ple_post_g = ple
        in_specs += [
            pl.BlockSpec((None, tm, p.shape[2]), lambda i: (layer, i, 0)),
            _resident((1, d)),
            _layer_weight(w_gate, layer, single_buffer=True),
            _layer_weight(w_proj, layer, single_buffer=True),
            _resident((1, d)),
        ]
        args += [p, ple_pre_g, w_gate, w_proj, ple_post_g]
    return pl.pallas_call(
        functools.partial(_ffn_kernel, tf=tf, with_ple=ple is not None),
        grid=(n // tm,),
        in_specs=in_specs,
        out_specs=pl.BlockSpec((tm, d), row),
        out_shape=jax.ShapeDtypeStruct((n, d), F32),
        scratch_shapes=[pltpu.VMEM((tm, d), F32)],
        compiler_params=_params("parallel"),
        name="ffn_ple" if ple is not None else "ffn",
    )(*args)


def _rope(y, cos, sin_signed):
    w = y.shape[1]
    reps = w // LANES
    cos = jnp.concatenate([cos] * reps, axis=1)
    sin_signed = jnp.concatenate([sin_signed] * reps, axis=1)
    lane = lax.broadcasted_iota(jnp.int32, y.shape, 1)
    first_half = (lane % DF_QK_DIM) < (DF_QK_DIM // 2)
    half = DF_QK_DIM // 2
    partner = jnp.where(first_half, pltpu.roll(y, w - half, 1), pltpu.roll(y, half, 1))
    return y * cos + partner * sin_signed


def _proj_kernel(h_ref, g_ref, w_ref, cos_ref, sin_ref,
                 sbq_ref, sbk_ref, sbv_ref, dfq_ref, dfk_ref, dfv_ref, gate_ref):
    u = _rms(h_ref[...], g_ref[...], NORM_EPS).astype(BF16)
    cw = SB_W
    sb_scale = LOG2E / math.sqrt(SB_HEAD_DIM)
    df_scale = LOG2E / math.sqrt(DF_QK_DIM)

    def cols(c):
        return _dot(u, w_ref[:, c * cw:(c + 1) * cw])

    sbq_ref[...] = (cols(0) * sb_scale).astype(BF16)
    sbk_ref[...] = cols(1).astype(BF16)
    sbv_ref[...] = cols(2).astype(BF16)
    cos = cos_ref[...]
    sin = sin_ref[...]
    dfq_ref[...] = (_rope(cols(3), cos, sin) * df_scale).astype(BF16)
    dfk_ref[...] = _rope(cols(4), cos, sin).astype(BF16)
    dfv_ref[...] = cols(5).astype(BF16)
    n_gate = gate_ref.shape[1] // cw
    for c in range(n_gate):
        gate_ref[:, c * cw:(c + 1) * cw] = jax.nn.sigmoid(cols(6 + c)).astype(BF16)


def _proj(h, g, w_in, layer, cos, sin):
    n, d = h.shape
    d_in = w_in.shape[2]
    tm = _pick_tile(n, PROJ_ROWS)
    row = lambda i: (i, 0)
    const = lambda i: (0, 0)
    n_gate_cols = d_in - 6 * SB_W
    head_out = jax.ShapeDtypeStruct((n, SB_W), BF16)
    return pl.pallas_call(
        _proj_kernel,
        grid=(n // tm,),
        in_specs=[
            pl.BlockSpec((tm, d), row),
            pl.BlockSpec((1, d), const),
            _layer_weight(w_in, layer),
            pl.BlockSpec((tm, LANES), row),
            pl.BlockSpec((tm, LANES), row),
        ],
        out_specs=[pl.BlockSpec((tm, SB_W), row)] * 6 + [pl.BlockSpec((tm, n_gate_cols), row)],
        out_shape=[head_out] * 6 + [jax.ShapeDtypeStruct((n, n_gate_cols), BF16)],
        compiler_params=_params("parallel"),
        name="proj_in",
    )(h, g, w_in, cos, sin)


def _stack_halves(q):
    lane = lax.broadcasted_iota(jnp.int32, q.shape, 1)
    lo = lane < (LANES // 2)
    zero = jnp.zeros_like(q)
    return jnp.concatenate([jnp.where(lo, q, zero), jnp.where(lo, zero, q)], axis=0)


def _group(ref, g, rows=None):
    cols = slice(g * LANES, (g + 1) * LANES)
    return ref[:, cols] if rows is None else ref[rows, cols]


def _sb_kernel(q_ref, k_ref, v_ref, o_ref, acc_ref, c_ref, s_ref, w_ref, z_ref, *, blk, groups):
    qi = pl.program_id(2)
    gs = range(groups)
    q2 = [_stack_halves(_group(q_ref, g)) for g in gs]
    r = lax.broadcasted_iota(jnp.int32, (2 * blk, blk), 0) % blk
    c = lax.broadcasted_iota(jnp.int32, (2 * blk, blk), 1)
    strict = c < r
    tri2 = (r >= c).astype(BF16)

    def rows_of(j):
        return pl.ds(pl.multiple_of(j * blk, blk), blk)

    def score(g, j):
        return _dot_nt(q2[g], _group(k_ref, g, rows_of(j)))

    def suffix_sum(z, masked):
        sp = jnp.log(1.0 + jnp.exp2(-jnp.abs(z))) * LOG2E
        n = jnp.maximum(z, 0.0) + sp
        if masked:
            n = jnp.where(strict, n, 0.0)
        hi = n.astype(BF16)
        lo = (n - hi.astype(F32)).astype(BF16)
        return _dot(jnp.concatenate([hi, lo], axis=1), tri2)

    def store_weights(slot, g, z, suffix, masked):
        wg = jnp.exp2(z - suffix)
        if masked:
            wg = jnp.where(strict, wg, 0.0)
        w_ref[slot, g] = wg.astype(BF16)
        carry = c_ref[g]
        s_ref[slot, g] = jnp.exp2(-carry)
        c_ref[g] = carry + suffix[:, :1]

    def accumulate(slot, g, j):
        acc_ref[g] += s_ref[slot, g] * _dot(w_ref[slot, g], _group(v_ref, g, rows_of(j)))

    def step(g, cur, j):
        z = z_ref[cur, g]
        suffix = suffix_sum(z, False)
        z_ref[1 - cur, g] = score(g, jnp.maximum(j - 1, 0))
        accumulate(cur, g, j + 1)
        store_weights(1 - cur, g, z, suffix, False)

    acc_ref[...] = jnp.zeros_like(acc_ref)
    c_ref[...] = jnp.zeros_like(c_ref)
    no_block = jnp.where(qi == 0, NO_BLOCK, 0.0)
    j1 = jnp.maximum(qi - 1, 0)
    z0 = [score(g, qi) for g in gs]
    z1 = [score(g, j1) for g in gs]
    for g in gs:
        suffix0 = suffix_sum(z0[g], True)
        suffix1 = suffix_sum(z1[g], False)
        store_weights(0, g, z0[g], suffix0, True)
        c_ref[g] += no_block
        accumulate(0, g, qi)
        store_weights(1, g, z1[g], suffix1, False)
        z_ref[1, g] = score(g, jnp.maximum(qi - 2, 0))

    def more(state):
        t, carry_min = state
        return jnp.logical_and(t < qi, carry_min < EXP2_UNDERFLOW)

    carry_min = [jnp.min(c_ref[g]) for g in gs]
    t_end = []
    for g in gs:
        def body(state, g=g):
            t, _ = state
            step(g, t % 2, qi - 1 - t)
            return t + 1, jnp.min(c_ref[g])

        t_end.append(lax.while_loop(more, body, (jnp.int32(1), carry_min[g]))[0])
    for g in gs:
        accumulate(t_end[g] % 2, g, jnp.maximum(qi - t_end[g], 0))

    lane = lax.broadcasted_iota(jnp.int32, (blk, LANES), 1)
    for g in range(groups):
        acc = acc_ref[g]
        o_ref[:, g * LANES:(g + 1) * LANES] = jnp.where(lane < LANES // 2, acc[:blk], acc[blk:]).astype(o_ref.dtype)


def _diff_kernel(lp_ref, g_ref, q_ref, k_ref, v_ref, o_ref, acc_ref, m_ref, l_ref, a_ref, p_ref, z_ref,
                 *, blk, groups, lam_init):
    qi = pl.program_id(2)
    gs = range(groups)
    q2 = [_stack_halves(_group(q_ref, g)) for g in gs]
    key = lax.broadcasted_iota(jnp.int32, (blk, 2 * blk), 0)
    qry = lax.broadcasted_iota(jnp.int32, (blk, 2 * blk), 1) % blk
    causal = key <= qry

    def rows_of(j):
        return pl.ds(pl.multiple_of(j * blk, blk), blk)

    def score(g, j):
        return _dot_nt(_group(k_ref, g, rows_of(j)), q2[g])

    def store_probs(slot, g, z, first):
        if first:
            z = jnp.where(causal, z, -jnp.inf)
        zmax = jnp.max(z, axis=0, keepdims=True)
        if first:
            m_new = zmax
        else:
            m_prev = m_ref[g]
            m_new = jnp.maximum(m_prev, zmax)
        p = jnp.exp2(z - m_new)
        psum = jnp.sum(p, axis=0, keepdims=True)
        p_ref[slot, g] = p.astype(BF16)
        if first:
            l_ref[g] = psum
            a_ref[slot, g] = jnp.ones((1, 2 * blk), F32)
        else:
            alpha = jnp.exp2(m_prev - m_new)
            l_ref[g] = alpha * l_ref[g] + psum
            a_ref[slot, g] = alpha
        m_ref[g] = m_new

    def accumulate(slot, g, j):
        pv = lax.dot_general(_group(v_ref, g, rows_of(j)), p_ref[slot, g], (((0,), (0,)), ((), ())),
                             preferred_element_type=F32)
        acc_ref[g] = a_ref[slot, g] * acc_ref[g] + pv

    def step(cur, j):
        for g in gs:
            store_probs(1 - cur, g, z_ref[cur, g], False)
            z_ref[1 - cur, g] = score(g, jnp.maximum(j - 1, 0))
            accumulate(cur, g, j + 1)

    acc_ref[...] = jnp.zeros_like(acc_ref)
    for g in gs:
        store_probs(0, g, score(g, qi), True)
        z_ref[0, g] = score(g, jnp.maximum(qi - 1, 0))

    def pair(t, _):
        j = qi - 1 - 2 * t
        step(0, j)
        step(1, j - 1)
        return 0

    lax.fori_loop(0, qi // 2, pair, 0)

    @pl.when(qi % 2 == 1)
    def _():
        step(0, 0)
        for g in gs:
            accumulate(1, g, 0)

    @pl.when(qi % 2 == 0)
    def _():
        for g in gs:
            accumulate(0, g, 0)

    lp = lp_ref[...]
    lam = (jnp.exp(jnp.sum(lp[0:1] * lp[1:2], axis=-1, keepdims=True))
           - jnp.exp(jnp.sum(lp[2:3] * lp[3:4], axis=-1, keepdims=True)) + lam_init)
    for g in range(groups):
        ot = acc_ref[g] / l_ref[g]
        o = (ot[:, :blk] - lam * ot[:, blk:]).T
        o_ref[:, g * LANES:(g + 1) * LANES] = (
            _rms(o, g_ref[...], SUBLN_EPS) * (1.0 - lam_init)).astype(o_ref.dtype)


def _attn_specs(s, blk, groups):
    w = groups * LANES
    q_spec = pl.BlockSpec((blk, w), lambda b, h, i: (b * (s // blk) + i, h))
    kv_spec = pl.BlockSpec((s, w), lambda b, h, i: (b, h))
    return q_spec, kv_spec


def _sb_attention(q, k, v, batch, s):
    n, w = q.shape
    blk = _pick_tile(s, ATTN_BLOCK)
    groups = SB_GROUPS_PER_STEP
    q_spec, kv_spec = _attn_specs(s, blk, groups)
    stat = pltpu.VMEM((groups, 2 * blk, LANES), F32)
    tile = (2, groups, 2 * blk, blk)
    return pl.pallas_call(
        functools.partial(_sb_kernel, blk=blk, groups=groups),
        grid=(batch, w // (groups * LANES), s // blk),
        in_specs=[q_spec, kv_spec, kv_spec],
        out_specs=q_spec,
        out_shape=jax.ShapeDtypeStruct((n, w), BF16),
        scratch_shapes=[stat, stat, pltpu.VMEM((2, groups, 2 * blk, LANES), F32),
                        pltpu.VMEM(tile, BF16), pltpu.VMEM(tile, F32)],
        compiler_params=_params("parallel", "parallel", "arbitrary"),
        name="sb_attn",
    )(q, k, v)


def _diff_attention(lp, g, q, k, v, batch, s, lam_init):
    n, w = q.shape
    blk = _pick_tile(s, ATTN_BLOCK)
    groups = DF_GROUPS_PER_STEP
    q_spec, kv_spec = _attn_specs(s, blk, groups)
    const = lambda b, h, i: (0, 0)
    stat = pltpu.VMEM((groups, 1, 2 * blk), F32)
    tile = (2, groups, blk, 2 * blk)
    return pl.pallas_call(
        functools.partial(_diff_kernel, blk=blk, groups=groups, lam_init=lam_init),
        grid=(batch, w // (groups * LANES), s // blk),
        in_specs=[pl.BlockSpec(lp.shape, const), pl.BlockSpec(g.shape, const), q_spec, kv_spec, kv_spec],
        out_specs=q_spec,
        out_shape=jax.ShapeDtypeStruct((n, w), BF16),
        scratch_shapes=[pltpu.VMEM((groups, LANES, 2 * blk), F32), stat, stat,
                        pltpu.VMEM((2, groups, 1, 2 * blk), F32),
                        pltpu.VMEM(tile, BF16), pltpu.VMEM(tile, F32)],
        compiler_params=_params("parallel", "parallel", "arbitrary"),
        name="diff_attn",
    )(lp, g, q, k, v)


def _merge_kernel(h_ref, osb_ref, odf_ref, gate_ref, wsb_ref, wdf_ref, wout_ref, g_ref, out_ref):
    d = h_ref.shape[1]
    y_sb = _dot(osb_ref[...], wsb_ref[...])
    y_df = _dot(odf_ref[...], wdf_ref[...])
    merged = gate_ref[:, :d].astype(F32) * y_sb + gate_ref[:, d:].astype(F32) * y_df
    out = _dot(merged.astype(BF16), wout_ref[...])
    out_ref[...] = h_ref[...] + _rms(out, g_ref[...], NORM_EPS)


def _merge(h, o_sb, o_df, gates, w_sb, w_df, w_out, layer, g):
    n, d = h.shape
    tm = _pick_tile(n, PROJ_ROWS)
    row = lambda i: (i, 0)
    const = lambda i: (0, 0)
    return pl.pallas_call(
        _merge_kernel,
        grid=(n // tm,),
        in_specs=[
            pl.BlockSpec((tm, d), row),
            pl.BlockSpec((tm, o_sb.shape[1]), row),
            pl.BlockSpec((tm, o_df.shape[1]), row),
            pl.BlockSpec((tm, gates.shape[1]), row),
            _layer_weight(w_sb, layer),
            _layer_weight(w_df, layer),
            _layer_weight(w_out, layer),
            pl.BlockSpec((1, d), const),
        ],
        out_specs=pl.BlockSpec((tm, d), row),
        out_shape=jax.ShapeDtypeStruct((n, d), F32),
        compiler_params=_params("parallel"),
        name="merge_out",
    )(h, o_sb, o_df, gates, w_sb, w_df, w_out, g)


def _rope_tables(positions):
    half = DF_QK_DIM // 2
    inv_freq = ROPE_THETA ** (-jnp.arange(half, dtype=F32) / half)
    ang = positions.astype(F32).reshape(-1, 1) * inv_freq
    cos, sin = jnp.cos(ang), jnp.sin(ang)
    reps = LANES // DF_QK_DIM
    return (jnp.tile(jnp.concatenate([cos, cos], axis=1), (1, reps)),
            jnp.tile(jnp.concatenate([-sin, sin], axis=1), (1, reps)))


def kernel(x, p, positions, ffn1_pre_g, ffn1_w_gu, ffn1_w_down, ffn1_post_g, mix_pre_g, w_in, diff_lambda, diff_subln_g, w_branch_sb, w_branch_diff, w_out, mix_post_g, ffn2_pre_g, ffn2_w_gu, ffn2_w_down, ffn2_post_g, ple_pre_g, w_ple_gate, w_ple_proj, ple_post_g):
    b, s, d = x.shape
    depth = p.shape[0]
    n = b * s
    h = x.reshape(n, d)
    p_rows = p.reshape(depth, n, -1)
    cos, sin = _rope_tables(positions)
    gain = lambda g, i: g[i].reshape(1, -1).astype(F32)
    (ffn1_w_gu, ffn1_w_down, w_in, w_branch_sb, w_branch_diff, w_out, ffn2_w_gu, ffn2_w_down,
     w_ple_gate, w_ple_proj) = (w.astype(BF16) for w in (
         ffn1_w_gu, ffn1_w_down, w_in, w_branch_sb, w_branch_diff, w_out, ffn2_w_gu, ffn2_w_down,
         w_ple_gate, w_ple_proj))
    for i in range(depth):
        h = _ffn(h, gain(ffn1_pre_g, i), ffn1_w_gu, ffn1_w_down, gain(ffn1_post_g, i), i)
        sbq, sbk, sbv, dfq, dfk, dfv, gates = _proj(h, gain(mix_pre_g, i), w_in, i, cos, sin)
        o_sb = _sb_attention(sbq, sbk, sbv, b, s)
        lam_init = 0.8 - 0.6 * math.exp(-0.3 * i)
        o_df = _diff_attention(diff_lambda[i].astype(F32), gain(diff_subln_g, i), dfq, dfk, dfv, b, s, lam_init)
        h = _merge(h, o_sb, o_df, gates, w_branch_sb, w_branch_diff, w_out, i, gain(mix_post_g, i))
        ple = (p_rows, gain(ple_pre_g, i), w_ple_gate, w_ple_proj, gain(ple_post_g, i))
        h = _ffn(h, gain(ffn2_pre_g, i), ffn2_w_gu, ffn2_w_down, gain(ffn2_post_g, i), i, ple=ple)
    return h.reshape(b, s, d)
```

```python
import functools
import math

import jax
import jax.numpy as jnp
from jax import lax
from jax.experimental import pallas as pl
from jax.experimental.pallas import tpu as pltpu

NORM_EPS = 1e-6
SUBLN_EPS = 1e-5
ROPE_THETA = 10000.0
SB_HEADS = 8
SB_HEAD_DIM = 64
DF_HEADS = 4
DF_QK_DIM = 64
DF_V_DIM = 2 * DF_QK_DIM
SB_W = SB_HEADS * SB_HEAD_DIM
DF_W = DF_HEADS * DF_V_DIM
assert DF_W == SB_W == DF_HEADS * 2 * DF_QK_DIM

LANES = 128
LOG2E = math.log2(math.e)
EXP2_UNDERFLOW = 160.0
NO_BLOCK = 1e6
MXU_WIDTH = 256
FFN_ROWS = 1024
FFN_COLS = MXU_WIDTH
PROJ_ROWS = 512
ATTN_BLOCK = MXU_WIDTH
SB_GROUPS_PER_STEP = 4
DF_GROUPS_PER_STEP = 4
VMEM_LIMIT_BYTES = 48 * 1024 * 1024

F32 = jnp.float32
BF16 = jnp.bfloat16


def _pick_tile(n, target):
    t = min(n, target)
    while n % t:
        t //= 2
    return t


def _rms(x, g, eps):
    return x * lax.rsqrt(jnp.mean(x * x, axis=-1, keepdims=True) + eps) * g


def _dot(a, b):
    return jnp.dot(a, b, preferred_element_type=F32)


def _dot_nt(a, b):
    return lax.dot_general(a, b, (((1,), (1,)), ((), ())), preferred_element_type=F32)


def _params(*sem):
    return pltpu.CompilerParams(dimension_semantics=sem, vmem_limit_bytes=VMEM_LIMIT_BYTES)


def _ffn_kernel(*refs, tf, with_ple):
    if with_ple:
        (h_ref, pre_g_ref, wgu_ref, wd_ref, post_g_ref,
         p_ref, ple_pre_g_ref, w_gate_ref, w_proj_ref, ple_post_g_ref, out_ref, acc_ref) = refs
    else:
        h_ref, pre_g_ref, wgu_ref, wd_ref, post_g_ref, out_ref, acc_ref = refs
    d_ff = wd_ref.shape[0]
    if with_ple:
        out_ref[...] = _dot(p_ref[...].astype(BF16), w_proj_ref[...])
    xn = _rms(h_ref[...], pre_g_ref[...], NORM_EPS).astype(BF16)
    for j in range(d_ff // tf):
        g = _dot(xn, wgu_ref[:, j * tf:(j + 1) * tf])
        u = _dot(xn, wgu_ref[:, d_ff + j * tf:d_ff + (j + 1) * tf])
        a = (g * jax.nn.sigmoid(g) * u).astype(BF16)
        f = _dot(a, wd_ref[j * tf:(j + 1) * tf, :])
        if j == 0:
            acc_ref[...] = f
        else:
            acc_ref[...] += f
    h = h_ref[...] + 0.5 * _rms(acc_ref[...], post_g_ref[...], NORM_EPS)
    if with_ple:
        hn = _rms(h, ple_pre_g_ref[...], NORM_EPS).astype(BF16)
        gate = jax.nn.sigmoid(_dot(hn, w_gate_ref[...]))
        e = out_ref[...] * gate
        h = h + _rms(e, ple_post_g_ref[...], NORM_EPS)
    out_ref[...] = h


def _resident(shape):
    return pl.BlockSpec(shape, lambda i: (0,) * len(shape), pipeline_mode=pl.Buffered(1))


def _layer_weight(w, layer, single_buffer=False):
    mode = dict(pipeline_mode=pl.Buffered(1)) if single_buffer else {}
    return pl.BlockSpec((None,) + w.shape[1:], lambda i: (layer, 0, 0), **mode)


def _ffn(h, pre_g, w_gu, w_down, post_g, layer, ple=None):
    n, d = h.shape
    d_ff = w_down.shape[1]
    tm = _pick_tile(n, FFN_ROWS)
    tf = FFN_COLS if d_ff % FFN_COLS == 0 else LANES
    row = lambda i: (i, 0)
    in_specs = [
        pl.BlockSpec((tm, d), row),
        _resident((1, d)),
        _layer_weight(w_gu, layer, single_buffer=True),
        _layer_weight(w_down, layer, single_buffer=True),
        _resident((1, d)),
    ]
    args = [h, pre_g, w_gu, w_down, post_g]
    if ple is not None:
        p, ple_pre_g, w_gate, w_proj, ple_post_g = ple
        in_specs += [
            pl.BlockSpec((None, tm, p.shape[2]), lambda i: (layer, i, 0)),
            _resident((1, d)),
            _layer_weight(w_gate, layer, single_buffer=True),
            _layer_weight(w_proj, layer, single_buffer=True),
            _resident((1, d)),
        ]
        args += [p, ple_pre_g, w_gate, w_proj, ple_post_g]
    return pl.pallas_call(
        functools.partial(_ffn_kernel, tf=tf, with_ple=ple is not None),
        grid=(n // tm,),
        in_specs=in_specs,
        out_specs=pl.BlockSpec((tm, d), row),
        out_shape=jax.ShapeDtypeStruct((n, d), F32),
        scratch_shapes=[pltpu.VMEM((tm, d), F32)],
        compiler_params=_params("parallel"),
        name="ffn_ple" if ple is not None else "ffn",
    )(*args)


def _rope(y, cos, sin_signed):
    w = y.shape[1]
    reps = w // LANES
    cos = jnp.concatenate([cos] * reps, axis=1)
    sin_signed = jnp.concatenate([sin_signed] * reps, axis=1)
    lane = lax.broadcasted_iota(jnp.int32, y.shape, 1)
    first_half = (lane % DF_QK_DIM) < (DF_QK_DIM // 2)
    half = DF_QK_DIM // 2
    partner = jnp.where(first_half, pltpu.roll(y, w - half, 1), pltpu.roll(y, half, 1))
    return y * cos + partner * sin_signed


def _proj_kernel(h_ref, g_ref, w_ref, cos_ref, sin_ref,
                 sbq_ref, sbk_ref, sbv_ref, dfq_ref, dfk_ref, dfv_ref, gate_ref):
    u = _rms(h_ref[...], g_ref[...], NORM_EPS).astype(BF16)
    cw = SB_W
    sb_scale = LOG2E / math.sqrt(SB_HEAD_DIM)
    df_scale = LOG2E / math.sqrt(DF_QK_DIM)

    def cols(c):
        return _dot(u, w_ref[:, c * cw:(c + 1) * cw])

    sbq_ref[...] = (cols(0) * sb_scale).astype(BF16)
    sbk_ref[...] = cols(1).astype(BF16)
    sbv_ref[...] = cols(2).astype(BF16)
    cos = cos_ref[...]
    sin = sin_ref[...]
    dfq_ref[...] = (_rope(cols(3), cos, sin) * df_scale).astype(BF16)
    dfk_ref[...] = _rope(cols(4), cos, sin).astype(BF16)
    dfv_ref[...] = cols(5).astype(BF16)
    n_gate = gate_ref.shape[1] // cw
    for c in range(n_gate):
        gate_ref[:, c * cw:(c + 1) * cw] = jax.nn.sigmoid(cols(6 + c)).astype(BF16)


def _proj(h, g, w_in, layer, cos, sin):
    n, d = h.shape
    d_in = w_in.shape[2]
    tm = _pick_tile(n, PROJ_ROWS)
    row = lambda i: (i, 0)
    const = lambda i: (0, 0)
    n_gate_cols = d_in - 6 * SB_W
    head_out = jax.ShapeDtypeStruct((n, SB_W), BF16)
    return pl.pallas_call(
        _proj_kernel,
        grid=(n // tm,),
        in_specs=[
            pl.BlockSpec((tm, d), row),
            pl.BlockSpec((1, d), const),
            _layer_weight(w_in, layer),
            pl.BlockSpec((tm, LANES), row),
            pl.BlockSpec((tm, LANES), row),
        ],
        out_specs=[pl.BlockSpec((tm, SB_W), row)] * 6 + [pl.BlockSpec((tm, n_gate_cols), row)],
        out_shape=[head_out] * 6 + [jax.ShapeDtypeStruct((n, n_gate_cols), BF16)],
        compiler_params=_params("parallel"),
        name="proj_in",
    )(h, g, w_in, cos, sin)


def _stack_halves(q):
    lane = lax.broadcasted_iota(jnp.int32, q.shape, 1)
    lo = lane < (LANES // 2)
    zero = jnp.zeros_like(q)
    return jnp.concatenate([jnp.where(lo, q, zero), jnp.where(lo, zero, q)], axis=0)


def _group(ref, g, rows=None):
    cols = slice(g * LANES, (g + 1) * LANES)
    return ref[:, cols] if rows is None else ref[rows, cols]


def _sb_kernel(q_ref, k_ref, v_ref, o_ref, acc_ref, c_ref, s_ref, w_ref, z_ref, *, blk, groups):
    qi = pl.program_id(2)
    gs = range(groups)
    q2 = [_stack_halves(_group(q_ref, g)) for g in gs]
    r = lax.broadcasted_iota(jnp.int32, (2 * blk, blk), 0) % blk
    c = lax.broadcasted_iota(jnp.int32, (2 * blk, blk), 1)
    strict = c < r
    tri2 = (r >= c).astype(BF16)

    def rows_of(j):
        return pl.ds(pl.multiple_of(j * blk, blk), blk)

    def score(g, j):
        return _dot_nt(q2[g], _group(k_ref, g, rows_of(j)))

    def suffix_sum(z, masked):
        sp = jnp.log(1.0 + jnp.exp2(-jnp.abs(z))) * LOG2E
        n = jnp.maximum(z, 0.0) + sp
        if masked:
            n = jnp.where(strict, n, 0.0)
        hi = n.astype(BF16)
        lo = (n - hi.astype(F32)).astype(BF16)
        return _dot(jnp.concatenate([hi, lo], axis=1), tri2)

    def store_weights(slot, g, z, suffix, masked):
        wg = jnp.exp2(z - suffix)
        if masked:
            wg = jnp.where(strict, wg, 0.0)
        w_ref[slot, g] = wg.astype(BF16)
        carry = c_ref[g]
        s_ref[slot, g] = jnp.exp2(-carry)
        c_ref[g] = carry + suffix[:, :1]

    def accumulate(slot, g, j):
        acc_ref[g] += s_ref[slot, g] * _dot(w_ref[slot, g], _group(v_ref, g, rows_of(j)))

    def step(g, cur, j):
        z = z_ref[cur, g]
        suffix = suffix_sum(z, False)
        z_ref[1 - cur, g] = score(g, jnp.maximum(j - 1, 0))
        accumulate(cur, g, j + 1)
        store_weights(1 - cur, g, z, suffix, False)

    acc_ref[...] = jnp.zeros_like(acc_ref)
    c_ref[...] = jnp.zeros_like(c_ref)
    no_block = jnp.where(qi == 0, NO_BLOCK, 0.0)
    j1 = jnp.maximum(qi - 1, 0)
    z0 = [score(g, qi) for g in gs]
    z1 = [score(g, j1) for g in gs]
    for g in gs:
        suffix0 = suffix_sum(z0[g], True)
        suffix1 = suffix_sum(z1[g], False)
        store_weights(0, g, z0[g], suffix0, True)
        c_ref[g] += no_block
        accumulate(0, g, qi)
        store_weights(1, g, z1[g], suffix1, False)
        z_ref[1, g] = score(g, jnp.maximum(qi - 2, 0))

    def more(state):
        t, carry_min = state
        return jnp.logical_and(t < qi, carry_min < EXP2_UNDERFLOW)

    carry_min = [jnp.min(c_ref[g]) for g in gs]
    t_end = []
    for g in gs:
        def body(state, g=g):
            t, _ = state
            step(g, t % 2, qi - 1 - t)
            return t + 1, jnp.min(c_ref[g])

        t_end.append(lax.while_loop(more, body, (jnp.int32(1), carry_min[g]))[0])
    for g in gs:
        accumulate(t_end[g] % 2, g, jnp.maximum(qi - t_end[g], 0))

    lane = lax.broadcasted_iota(jnp.int32, (blk, LANES), 1)
    for g in range(groups):
        acc = acc_ref[g]
        o_ref[:, g * LANES:(g + 1) * LANES] = jnp.where(lane < LANES // 2, acc[:blk], acc[blk:]).astype(o_ref.dtype)


def _diff_kernel(lp_ref, g_ref, q_ref, k_ref, v_ref, o_ref, acc_ref, m_ref, l_ref, a_ref, p_ref, z_ref,
                 *, blk, groups, lam_init):
    qi = pl.program_id(2)
    gs = range(groups)
    q2 = [_stack_halves(_group(q_ref, g)) for g in gs]
    key = lax.broadcasted_iota(jnp.int32, (blk, 2 * blk), 0)
    qry = lax.broadcasted_iota(jnp.int32, (blk, 2 * blk), 1) % blk
    causal = key <= qry

    def rows_of(j):
        return pl.ds(pl.multiple_of(j * blk, blk), blk)

    def score(g, j):
        return _dot_nt(_group(k_ref, g, rows_of(j)), q2[g])

    def store_probs(slot, g, z, first):
        if first:
            z = jnp.where(causal, z, -jnp.inf)
        zmax = jnp.max(z, axis=0, keepdims=True)
        if first:
            m_new = zmax
        else:
            m_prev = m_ref[g]
            m_new = jnp.maximum(m_prev, zmax)
        p = jnp.exp2(z - m_new)
        psum = jnp.sum(p, axis=0, keepdims=True)
        p_ref[slot, g] = p.astype(BF16)
        if first:
            l_ref[g] = psum
            a_ref[slot, g] = jnp.ones((1, 2 * blk), F32)
        else:
            alpha = jnp.exp2(m_prev - m_new)
            l_ref[g] = alpha * l_ref[g] + psum
            a_ref[slot, g] = alpha
        m_ref[g] = m_new

    def accumulate(slot, g, j):
        pv = lax.dot_general(_group(v_ref, g, rows_of(j)), p_ref[slot, g], (((0,), (0,)), ((), ())),
                             preferred_element_type=F32)
        acc_ref[g] = a_ref[slot, g] * acc_ref[g] + pv

    def step(cur, j):
        for g in gs:
            store_probs(1 - cur, g, z_ref[cur, g], False)
            z_ref[1 - cur, g] = score(g, jnp.maximum(j - 1, 0))
            accumulate(cur, g, j + 1)

    acc_ref[...] = jnp.zeros_like(acc_ref)
    for g in gs:
        store_probs(0, g, score(g, qi), True)
        z_ref[0, g] = score(g, jnp.maximum(qi - 1, 0))

    def pair(t, _):
        j = qi - 1 - 2 * t
        step(0, j)
        step(1, j - 1)
        return 0

    lax.fori_loop(0, qi // 2, pair, 0)

    @pl.when(qi % 2 == 1)
    def _():
        step(0, 0)
        for g in gs:
            accumulate(1, g, 0)

    @pl.when(qi % 2 == 0)
    def _():
        for g in gs:
            accumulate(0, g, 0)

    lp = lp_ref[...]
    lam = (jnp.exp(jnp.sum(lp[0:1] * lp[1:2], axis=-1, keepdims=True))
           - jnp.exp(jnp.sum(lp[2:3] * lp[3:4], axis=-1, keepdims=True)) + lam_init)
    for g in range(groups):
        ot = acc_ref[g] / l_ref[g]
        o = (ot[:, :blk] - lam * ot[:, blk:]).T
        o_ref[:, g * LANES:(g + 1) * LANES] = (
            _rms(o, g_ref[...], SUBLN_EPS) * (1.0 - lam_init)).astype(o_ref.dtype)


def _attn_specs(s, blk, groups):
    w = groups * LANES
    q_spec = pl.BlockSpec((blk, w), lambda b, h, i: (b * (s // blk) + i, h))
    kv_spec = pl.BlockSpec((s, w), lambda b, h, i: (b, h))
    return q_spec, kv_spec


def _sb_attention(q, k, v, batch, s):
    n, w = q.shape
    blk = _pick_tile(s, ATTN_BLOCK)
    groups = SB_GROUPS_PER_STEP
    q_spec, kv_spec = _attn_specs(s, blk, groups)
    stat = pltpu.VMEM((groups, 2 * blk, LANES), F32)
    tile = (2, groups, 2 * blk, blk)
    return pl.pallas_call(
        functools.partial(_sb_kernel, blk=blk, groups=groups),
        grid=(batch, w // (groups * LANES), s // blk),
        in_specs=[q_spec, kv_spec, kv_spec],
        out_specs=q_spec,
        out_shape=jax.ShapeDtypeStruct((n, w), BF16),
        scratch_shapes=[stat, stat, pltpu.VMEM((2, groups, 2 * blk, LANES), F32),
                        pltpu.VMEM(tile, BF16), pltpu.VMEM(tile, F32)],
        compiler_params=_params("parallel", "parallel", "arbitrary"),
        name="sb_attn",
    )(q, k, v)


def _diff_attention(lp, g, q, k, v, batch, s, lam_init):
    n, w = q.shape
    blk = _pick_tile(s, ATTN_BLOCK)
    groups = DF_GROUPS_PER_STEP
    q_spec, kv_spec = _attn_specs(s, blk, groups)
    const = lambda b, h, i: (0, 0)
    stat = pltpu.VMEM((groups, 1, 2 * blk), F32)
    tile = (2, groups, blk, 2 * blk)
    return pl.pallas_call(
        functools.partial(_diff_kernel, blk=blk, groups=groups, lam_init=lam_init),
        grid=(batch, w // (groups * LANES), s // blk),
        in_specs=[pl.BlockSpec(lp.shape, const), pl.BlockSpec(g.shape, const), q_spec, kv_spec, kv_spec],
        out_specs=q_spec,
        out_shape=jax.ShapeDtypeStruct((n, w), BF16),
        scratch_shapes=[pltpu.VMEM((groups, LANES, 2 * blk), F32), stat, stat,
                        pltpu.VMEM((2, groups, 1, 2 * blk), F32),
                        pltpu.VMEM(tile, BF16), pltpu.VMEM(tile, F32)],
        compiler_params=_params("parallel", "parallel", "arbitrary"),
        name="diff_attn",
    )(lp, g, q, k, v)


def _merge_kernel(h_ref, osb_ref, odf_ref, gate_ref, wsb_ref, wdf_ref, wout_ref, g_ref, out_ref):
    d = h_ref.shape[1]
    y_sb = _dot(osb_ref[...], wsb_ref[...])
    y_df = _dot(odf_ref[...], wdf_ref[...])
    merged = gate_ref[:, :d].astype(F32) * y_sb + gate_ref[:, d:].astype(F32) * y_df
    out = _dot(merged.astype(BF16), wout_ref[...])
    out_ref[...] = h_ref[...] + _rms(out, g_ref[...], NORM_EPS)


def _merge(h, o_sb, o_df, gates, w_sb, w_df, w_out, layer, g):
    n, d = h.shape
    tm = _pick_tile(n, PROJ_ROWS)
    row = lambda i: (i, 0)
    const = lambda i: (0, 0)
    return pl.pallas_call(
        _merge_kernel,
        grid=(n // tm,),
        in_specs=[
            pl.BlockSpec((tm, d), row),
            pl.BlockSpec((tm, o_sb.shape[1]), row),
            pl.BlockSpec((tm, o_df.shape[1]), row),
            pl.BlockSpec((tm, gates.shape[1]), row),
            _layer_weight(w_sb, layer),
            _layer_weight(w_df, layer),
            _layer_weight(w_out, layer),
            pl.BlockSpec((1, d), const),
        ],
        out_specs=pl.BlockSpec((tm, d), row),
        out_shape=jax.ShapeDtypeStruct((n, d), F32),
        compiler_params=_params("parallel"),
        name="merge_out",
    )(h, o_sb, o_df, gates, w_sb, w_df, w_out, g)


def _rope_tables(positions):
    half = DF_QK_DIM // 2
    inv_freq = ROPE_THETA ** (-jnp.arange(half, dtype=F32) / half)
    ang = positions.astype(F32).reshape(-1, 1) * inv_freq
    cos, sin = jnp.cos(ang), jnp.sin(ang)
    reps = LANES // DF_QK_DIM
    return (jnp.tile(jnp.concatenate([cos, cos], axis=1), (1, reps)),
            jnp.tile(jnp.concatenate([-sin, sin], axis=1), (1, reps)))


def kernel(x, p, positions, ffn1_pre_g, ffn1_w_gu, ffn1_w_down, ffn1_post_g, mix_pre_g, w_in, diff_lambda, diff_subln_g, w_branch_sb, w_branch_diff, w_out, mix_post_g, ffn2_pre_g, ffn2_w_gu, ffn2_w_down, ffn2_post_g, ple_pre_g, w_ple_gate, w_ple_proj, ple_post_g):
    b, s, d = x.shape
    depth = p.shape[0]
    n = b * s
    h = x.reshape(n, d)
    p_rows = p.reshape(depth, n, -1)
    cos, sin = _rope_tables(positions)
    gain = lambda g, i: g[i].reshape(1, -1).astype(F32)
    (ffn1_w_gu, ffn1_w_down, w_in, w_branch_sb, w_branch_diff, w_out, ffn2_w_gu, ffn2_w_down,
     w_ple_gate, w_ple_proj) = (w.astype(BF16) for w in (
         ffn1_w_gu, ffn1_w_down, w_in, w_branch_sb, w_branch_diff, w_out, ffn2_w_gu, ffn2_w_down,
         w_ple_gate, w_ple_proj))
    for i in range(depth):
        h = _ffn(h, gain(ffn1_pre_g, i), ffn1_w_gu, ffn1_w_down, gain(ffn1_post_g, i), i)
        sbq, sbk, sbv, dfq, dfk, dfv, gates = _proj(h, gain(mix_pre_g, i), w_in, i, cos, sin)
        o_sb = _sb_attention(sbq, sbk, sbv, b, s)
        lam_init = 0.8 - 0.6 * math.exp(-0.3 * i)
        o_df = _diff_attention(diff_lambda[i].astype(F32), gain(diff_subln_g, i), dfq, dfk, dfv, b, s, lam_init)
        h = _merge(h, o_sb, o_df, gates, w_branch_sb, w_branch_diff, w_out, i, gain(mix_post_g, i))
        ple = (p_rows, gain(ple_pre_g, i), w_ple_gate, w_ple_proj, gain(ple_post_g, i))
        h = _ffn(h, gain(ffn2_pre_g, i), ffn2_w_gu, ffn2_w_down, gain(ffn2_post_g, i), i, ple=ple)
    return h.reshape(b, s, d)
```

```python
import functools
import math

import jax
import jax.numpy as jnp
from jax import lax
from jax.experimental import pallas as pl
from jax.experimental.pallas import tpu as pltpu

NORM_EPS = 1e-6
SUBLN_EPS = 1e-5
ROPE_THETA = 10000.0
SB_HEADS = 8
SB_HEAD_DIM = 64
DF_HEADS = 4
DF_QK_DIM = 64
DF_V_DIM = 2 * DF_QK_DIM
SB_W = SB_HEADS * SB_HEAD_DIM
DF_W = DF_HEADS * DF_V_DIM
assert DF_W == SB_W == DF_HEADS * 2 * DF_QK_DIM

LANES = 128
LOG2E = math.log2(math.e)
EXP2_UNDERFLOW = 160.0
NO_BLOCK = 1e6
MXU_WIDTH = 256
FFN_ROWS = 1024
FFN_PARTS = 2
FFN_COLS = MXU_WIDTH
PROJ_ROWS = 512
ATTN_BLOCK = MXU_WIDTH
SB_GROUPS_PER_STEP = 4
DF_GROUPS_PER_STEP = 4
VMEM_LIMIT_BYTES = 48 * 1024 * 1024

F32 = jnp.float32
BF16 = jnp.bfloat16


def _pick_tile(n, target):
    t = min(n, target)
    while n % t:
        t //= 2
    return t


def _rms(x, g, eps):
    return x * lax.rsqrt(jnp.mean(x * x, axis=-1, keepdims=True) + eps) * g


def _dot(a, b):
    return jnp.dot(a, b, preferred_element_type=F32)


def _dot_nt(a, b):
    return lax.dot_general(a, b, (((1,), (1,)), ((), ())), preferred_element_type=F32)


def _params(*sem):
    return pltpu.CompilerParams(dimension_semantics=sem, vmem_limit_bytes=VMEM_LIMIT_BYTES)


def _ffn_kernel(*refs, tf, with_ple):
    if with_ple:
        (h_ref, pre_g_ref, wgu_ref, wd_ref, post_g_ref,
         p_ref, ple_pre_g_ref, w_gate_ref, w_proj_ref, ple_post_g_ref, out_ref, acc_ref) = refs
    else:
        h_ref, pre_g_ref, wgu_ref, wd_ref, post_g_ref, out_ref, acc_ref = refs
    d_ff = wd_ref.shape[0]
    if with_ple:
        out_ref[...] = _dot(p_ref[...].astype(BF16), w_proj_ref[...])
    n_chunks = d_ff // tf
    parts = 1 if with_ple else FFN_PARTS
    part = h_ref.shape[0] // parts
    rows = [slice(r * part, (r + 1) * part) for r in range(parts)]
    xn = [None] * parts

    def prologue(r):
        xn[r] = _rms(h_ref[rows[r]], pre_g_ref[...], NORM_EPS).astype(BF16)

    def chunk(r, j):
        g = _dot(xn[r], wgu_ref[:, j * tf:(j + 1) * tf])
        u = _dot(xn[r], wgu_ref[:, d_ff + j * tf:d_ff + (j + 1) * tf])
        a = (g * jax.nn.sigmoid(g) * u).astype(BF16)
        f = _dot(a, wd_ref[j * tf:(j + 1) * tf, :])
        if j == 0:
            acc_ref[rows[r]] = f
        else:
            acc_ref[rows[r]] += f

    def epilogue(r):
        h = h_ref[rows[r]] + 0.5 * _rms(acc_ref[rows[r]], post_g_ref[...], NORM_EPS)
        if with_ple:
            hn = _rms(h, ple_pre_g_ref[...], NORM_EPS).astype(BF16)
            gate = jax.nn.sigmoid(_dot(hn, w_gate_ref[...]))
            e = out_ref[rows[r]] * gate
            h = h + _rms(e, ple_post_g_ref[...], NORM_EPS)
        out_ref[rows[r]] = h

    for t in range(n_chunks + parts - 1):
        for r in range(parts):
            j = t - r
            if j == 0:
                prologue(r)
            if 0 <= j < n_chunks:
                chunk(r, j)
            if j == n_chunks - 1:
                epilogue(r)


def _resident(shape):
    return pl.BlockSpec(shape, lambda i: (0,) * len(shape), pipeline_mode=pl.Buffered(1))


def _layer_weight(w, layer, single_buffer=False):
    mode = dict(pipeline_mode=pl.Buffered(1)) if single_buffer else {}
    return pl.BlockSpec((None,) + w.shape[1:], lambda i: (layer, 0, 0), **mode)


def _ffn(h, pre_g, w_gu, w_down, post_g, layer, ple=None):
    n, d = h.shape
    d_ff = w_down.shape[1]
    tm = _pick_tile(n, FFN_ROWS)
    tf = FFN_COLS if d_ff % FFN_COLS == 0 else LANES
    row = lambda i: (i, 0)
    in_specs = [
        pl.BlockSpec((tm, d), row),
        _resident((1, d)),
        _layer_weight(w_gu, layer, single_buffer=True),
        _layer_weight(w_down, layer, single_buffer=True),
        _resident((1, d)),
    ]
    args = [h, pre_g, w_gu, w_down, post_g]
    if ple is not None:
        p, ple_pre_g, w_gate, w_proj, ple_post_g = ple
        in_specs += [
            pl.BlockSpec((None, tm, p.shape[2]), lambda i: (layer, i, 0)),
            _resident((1, d)),
            _layer_weight(w_gate, layer, single_buffer=True),
            _layer_weight(w_proj, layer, single_buffer=True),
            _resident((1, d)),
        ]
        args += [p, ple_pre_g, w_gate, w_proj, ple_post_g]
    return pl.pallas_call(
        functools.partial(_ffn_kernel, tf=tf, with_ple=ple is not None),
        grid=(n // tm,),
        in_specs=in_specs,
        out_specs=pl.BlockSpec((tm, d), row),
        out_shape=jax.ShapeDtypeStruct((n, d), F32),
        scratch_shapes=[pltpu.VMEM((tm, d), F32)],
        compiler_params=_params("parallel"),
        name="ffn_ple" if ple is not None else "ffn",
    )(*args)


def _rope(y, cos, sin_signed):
    w = y.shape[1]
    reps = w // LANES
    cos = jnp.concatenate([cos] * reps, axis=1)
    sin_signed = jnp.concatenate([sin_signed] * reps, axis=1)
    lane = lax.broadcasted_iota(jnp.int32, y.shape, 1)
    first_half = (lane % DF_QK_DIM) < (DF_QK_DIM // 2)
    half = DF_QK_DIM // 2
    partner = jnp.where(first_half, pltpu.roll(y, w - half, 1), pltpu.roll(y, half, 1))
    return y * cos + partner * sin_signed


def _proj_kernel(h_ref, g_ref, w_ref, cos_ref, sin_ref,
                 sbq_ref, sbk_ref, sbv_ref, dfq_ref, dfk_ref, dfv_ref, gate_ref):
    u = _rms(h_ref[...], g_ref[...], NORM_EPS).astype(BF16)
    cw = SB_W
    sb_scale = LOG2E / math.sqrt(SB_HEAD_DIM)
    df_scale = LOG2E / math.sqrt(DF_QK_DIM)

    def cols(c):
        return _dot(u, w_ref[:, c * cw:(c + 1) * cw])

    sbq_ref[...] = (cols(0) * sb_scale).astype(BF16)
    sbk_ref[...] = cols(1).astype(BF16)
    sbv_ref[...] = cols(2).astype(BF16)
    cos = cos_ref[...]
    sin = sin_ref[...]
    dfq_ref[...] = (_rope(cols(3), cos, sin) * df_scale).astype(BF16)
    dfk_ref[...] = _rope(cols(4), cos, sin).astype(BF16)
    dfv_ref[...] = cols(5).astype(BF16)
    n_gate = gate_ref.shape[1] // cw
    for c in range(n_gate):
        gate_ref[:, c * cw:(c + 1) * cw] = jax.nn.sigmoid(cols(6 + c)).astype(BF16)


def _proj(h, g, w_in, layer, cos, sin):
    n, d = h.shape
    d_in = w_in.shape[2]
    tm = _pick_tile(n, PROJ_ROWS)
    row = lambda i: (i, 0)
    const = lambda i: (0, 0)
    n_gate_cols = d_in - 6 * SB_W
    head_out = jax.ShapeDtypeStruct((n, SB_W), BF16)
    return pl.pallas_call(
        _proj_kernel,
        grid=(n // tm,),
        in_specs=[
            pl.BlockSpec((tm, d), row),
            pl.BlockSpec((1, d), const),
            _layer_weight(w_in, layer),
            pl.BlockSpec((tm, LANES), row),
            pl.BlockSpec((tm, LANES), row),
        ],
        out_specs=[pl.BlockSpec((tm, SB_W), row)] * 6 + [pl.BlockSpec((tm, n_gate_cols), row)],
        out_shape=[head_out] * 6 + [jax.ShapeDtypeStruct((n, n_gate_cols), BF16)],
        compiler_params=_params("parallel"),
        name="proj_in",
    )(h, g, w_in, cos, sin)


def _stack_halves(q):
    lane = lax.broadcasted_iota(jnp.int32, q.shape, 1)
    lo = lane < (LANES // 2)
    zero = jnp.zeros_like(q)
    return jnp.concatenate([jnp.where(lo, q, zero), jnp.where(lo, zero, q)], axis=0)


def _group(ref, g, rows=None):
    cols = slice(g * LANES, (g + 1) * LANES)
    return ref[:, cols] if rows is None else ref[rows, cols]


def _sb_kernel(q_ref, k_ref, v_ref, o_ref, acc_ref, c_ref, s_ref, w_ref, z_ref, *, blk, groups):
    qi = pl.program_id(2)
    gs = range(groups)
    q2 = [_stack_halves(_group(q_ref, g)) for g in gs]
    r = lax.broadcasted_iota(jnp.int32, (2 * blk, blk), 0) % blk
    c = lax.broadcasted_iota(jnp.int32, (2 * blk, blk), 1)
    strict = c < r
    tri2 = (r >= c).astype(BF16)

    def rows_of(j):
        return pl.ds(pl.multiple_of(j * blk, blk), blk)

    def score(g, j):
        return _dot_nt(q2[g], _group(k_ref, g, rows_of(j)))

    def suffix_sum(z, masked):
        sp = jnp.log(1.0 + jnp.exp2(-jnp.abs(z))) * LOG2E
        n = jnp.maximum(z, 0.0) + sp
        if masked:
            n = jnp.where(strict, n, 0.0)
        hi = n.astype(BF16)
        lo = (n - hi.astype(F32)).astype(BF16)
        return _dot(jnp.concatenate([hi, lo], axis=1), tri2)

    def store_weights(slot, g, z, suffix, masked):
        wg = jnp.exp2(z - suffix)
        if masked:
            wg = jnp.where(strict, wg, 0.0)
        w_ref[slot, g] = wg.astype(BF16)
        carry = c_ref[g]
        s_ref[slot, g] = jnp.exp2(-carry)
        c_ref[g] = carry + suffix[:, :1]

    def accumulate(slot, g, j):
        acc_ref[g] += s_ref[slot, g] * _dot(w_ref[slot, g], _group(v_ref, g, rows_of(j)))

    def step(g, cur, j):
        z = z_ref[cur, g]
        suffix = suffix_sum(z, False)
        z_ref[1 - cur, g] = score(g, jnp.maximum(j - 1, 0))
        accumulate(cur, g, j + 1)
        store_weights(1 - cur, g, z, suffix, False)

    acc_ref[...] = jnp.zeros_like(acc_ref)
    c_ref[...] = jnp.zeros_like(c_ref)
    no_block = jnp.where(qi == 0, NO_BLOCK, 0.0)
    j1 = jnp.maximum(qi - 1, 0)
    z0 = [score(g, qi) for g in gs]
    z1 = [score(g, j1) for g in gs]
    for g in gs:
        suffix0 = suffix_sum(z0[g], True)
        suffix1 = suffix_sum(z1[g], False)
        store_weights(0, g, z0[g], suffix0, True)
        c_ref[g] += no_block
        accumulate(0, g, qi)
        store_weights(1, g, z1[g], suffix1, False)
        z_ref[1, g] = score(g, jnp.maximum(qi - 2, 0))

    def more(state):
        t, carry_min = state
        return jnp.logical_and(t < qi, carry_min < EXP2_UNDERFLOW)

    carry_min = [jnp.min(c_ref[g]) for g in gs]
    t_end = []
    for g in gs:
        def body(state, g=g):
            t, _ = state
            step(g, t % 2, qi - 1 - t)
            return t + 1, jnp.min(c_ref[g])

        t_end.append(lax.while_loop(more, body, (jnp.int32(1), carry_min[g]))[0])
    for g in gs:
        accumulate(t_end[g] % 2, g, jnp.maximum(qi - t_end[g], 0))

    lane = lax.broadcasted_iota(jnp.int32, (blk, LANES), 1)
    for g in range(groups):
        acc = acc_ref[g]
        o_ref[:, g * LANES:(g + 1) * LANES] = jnp.where(lane < LANES // 2, acc[:blk], acc[blk:]).astype(o_ref.dtype)


def _diff_kernel(lp_ref, g_ref, q_ref, k_ref, v_ref, o_ref, acc_ref, m_ref, l_ref, a_ref, p_ref, z_ref,
                 *, blk, groups, lam_init):
    qi = pl.program_id(2)
    gs = range(groups)
    q2 = [_stack_halves(_group(q_ref, g)) for g in gs]
    key = lax.broadcasted_iota(jnp.int32, (blk, 2 * blk), 0)
    qry = lax.broadcasted_iota(jnp.int32, (blk, 2 * blk), 1) % blk
    causal = key <= qry

    def rows_of(j):
        return pl.ds(pl.multiple_of(j * blk, blk), blk)

    def score(g, j):
        return _dot_nt(_group(k_ref, g, rows_of(j)), q2[g])

    def store_probs(slot, g, z, first):
        if first:
            z = jnp.where(causal, z, -jnp.inf)
        zmax = jnp.max(z, axis=0, keepdims=True)
        if first:
            m_new = zmax
        else:
            m_prev = m_ref[g]
            m_new = jnp.maximum(m_prev, zmax)
        p = jnp.exp2(z - m_new)
        psum = jnp.sum(p, axis=0, keepdims=True)
        p_ref[slot, g] = p.astype(BF16)
        if first:
            l_ref[g] = psum
            a_ref[slot, g] = jnp.ones((1, 2 * blk), F32)
        else:
            alpha = jnp.exp2(m_prev - m_new)
            l_ref[g] = alpha * l_ref[g] + psum
            a_ref[slot, g] = alpha
        m_ref[g] = m_new

    def accumulate(slot, g, j):
        pv = lax.dot_general(_group(v_ref, g, rows_of(j)), p_ref[slot, g], (((0,), (0,)), ((), ())),
                             preferred_element_type=F32)
        acc_ref[g] = a_ref[slot, g] * acc_ref[g] + pv

    def step(cur, j):
        for g in gs:
            store_probs(1 - cur, g, z_ref[cur, g], False)
            z_ref[1 - cur, g] = score(g, jnp.maximum(j - 1, 0))
            accumulate(cur, g, j + 1)

    acc_ref[...] = jnp.zeros_like(acc_ref)
    for g in gs:
        store_probs(0, g, score(g, qi), True)
        z_ref[0, g] = score(g, jnp.maximum(qi - 1, 0))

    def pair(t, _):
        j = qi - 1 - 2 * t
        step(0, j)
        step(1, j - 1)
        return 0

    lax.fori_loop(0, qi // 2, pair, 0)

    @pl.when(qi % 2 == 1)
    def _():
        step(0, 0)
        for g in gs:
            accumulate(1, g, 0)

    @pl.when(qi % 2 == 0)
    def _():
        for g in gs:
            accumulate(0, g, 0)

    lp = lp_ref[...]
    lam = (jnp.exp(jnp.sum(lp[0:1] * lp[1:2], axis=-1, keepdims=True))
           - jnp.exp(jnp.sum(lp[2:3] * lp[3:4], axis=-1, keepdims=True)) + lam_init)
    for g in range(groups):
        ot = acc_ref[g] / l_ref[g]
        o = (ot[:, :blk] - lam * ot[:, blk:]).T
        o_ref[:, g * LANES:(g + 1) * LANES] = (
            _rms(o, g_ref[...], SUBLN_EPS) * (1.0 - lam_init)).astype(o_ref.dtype)


def _attn_specs(s, blk, groups):
    w = groups * LANES
    q_spec = pl.BlockSpec((blk, w), lambda b, h, i: (b * (s // blk) + i, h))
    kv_spec = pl.BlockSpec((s, w), lambda b, h, i: (b, h))
    return q_spec, kv_spec


def _sb_attention(q, k, v, batch, s):
    n, w = q.shape
    blk = _pick_tile(s, ATTN_BLOCK)
    groups = SB_GROUPS_PER_STEP
    q_spec, kv_spec = _attn_specs(s, blk, groups)
    stat = pltpu.VMEM((groups, 2 * blk, LANES), F32)
    tile = (2, groups, 2 * blk, blk)
    return pl.pallas_call(
        functools.partial(_sb_kernel, blk=blk, groups=groups),
        grid=(batch, w // (groups * LANES), s // blk),
        in_specs=[q_spec, kv_spec, kv_spec],
        out_specs=q_spec,
        out_shape=jax.ShapeDtypeStruct((n, w), BF16),
        scratch_shapes=[stat, stat, pltpu.VMEM((2, groups, 2 * blk, LANES), F32),
                        pltpu.VMEM(tile, BF16), pltpu.VMEM(tile, F32)],
        compiler_params=_params("parallel", "parallel", "arbitrary"),
        name="sb_attn",
    )(q, k, v)


def _diff_attention(lp, g, q, k, v, batch, s, lam_init):
    n, w = q.shape
    blk = _pick_tile(s, ATTN_BLOCK)
    groups = DF_GROUPS_PER_STEP
    q_spec, kv_spec = _attn_specs(s, blk, groups)
    const = lambda b, h, i: (0, 0)
    stat = pltpu.VMEM((groups, 1, 2 * blk), F32)
    tile = (2, groups, blk, 2 * blk)
    return pl.pallas_call(
        functools.partial(_diff_kernel, blk=blk, groups=groups, lam_init=lam_init),
        grid=(batch, w // (groups * LANES), s // blk),
        in_specs=[pl.BlockSpec(lp.shape, const), pl.BlockSpec(g.shape, const), q_spec, kv_spec, kv_spec],
        out_specs=q_spec,
        out_shape=jax.ShapeDtypeStruct((n, w), BF16),
        scratch_shapes=[pltpu.VMEM((groups, LANES, 2 * blk), F32), stat, stat,
                        pltpu.VMEM((2, groups, 1, 2 * blk), F32),
                        pltpu.VMEM(tile, BF16), pltpu.VMEM(tile, F32)],
        compiler_params=_params("parallel", "parallel", "arbitrary"),
        name="diff_attn",
    )(lp, g, q, k, v)


def _merge_kernel(h_ref, osb_ref, odf_ref, gate_ref, wsb_ref, wdf_ref, wout_ref, g_ref, out_ref):
    d = h_ref.shape[1]
    y_sb = _dot(osb_ref[...], wsb_ref[...])
    y_df = _dot(odf_ref[...], wdf_ref[...])
    merged = gate_ref[:, :d].astype(F32) * y_sb + gate_ref[:, d:].astype(F32) * y_df
    out = _dot(merged.astype(BF16), wout_ref[...])
    out_ref[...] = h_ref[...] + _rms(out, g_ref[...], NORM_EPS)


def _merge(h, o_sb, o_df, gates, w_sb, w_df, w_out, layer, g):
    n, d = h.shape
    tm = _pick_tile(n, PROJ_ROWS)
    row = lambda i: (i, 0)
    const = lambda i: (0, 0)
    return pl.pallas_call(
        _merge_kernel,
        grid=(n // tm,),
        in_specs=[
            pl.BlockSpec((tm, d), row),
            pl.BlockSpec((tm, o_sb.shape[1]), row),
            pl.BlockSpec((tm, o_df.shape[1]), row),
            pl.BlockSpec((tm, gates.shape[1]), row),
            _layer_weight(w_sb, layer),
            _layer_weight(w_df, layer),
            _layer_weight(w_out, layer),
            pl.BlockSpec((1, d), const),
        ],
        out_specs=pl.BlockSpec((tm, d), row),
        out_shape=jax.ShapeDtypeStruct((n, d), F32),
        compiler_params=_params("parallel"),
        name="merge_out",
    )(h, o_sb, o_df, gates, w_sb, w_df, w_out, g)


def _rope_tables(positions):
    half = DF_QK_DIM // 2
    inv_freq = ROPE_THETA ** (-jnp.arange(half, dtype=F32) / half)
    ang = positions.astype(F32).reshape(-1, 1) * inv_freq
    cos, sin = jnp.cos(ang), jnp.sin(ang)
    reps = LANES // DF_QK_DIM
    return (jnp.tile(jnp.concatenate([cos, cos], axis=1), (1, reps)),
            jnp.tile(jnp.concatenate([-sin, sin], axis=1), (1, reps)))


def kernel(x, p, positions, ffn1_pre_g, ffn1_w_gu, ffn1_w_down, ffn1_post_g, mix_pre_g, w_in, diff_lambda, diff_subln_g, w_branch_sb, w_branch_diff, w_out, mix_post_g, ffn2_pre_g, ffn2_w_gu, ffn2_w_down, ffn2_post_g, ple_pre_g, w_ple_gate, w_ple_proj, ple_post_g):
    b, s, d = x.shape
    depth = p.shape[0]
    n = b * s
    h = x.reshape(n, d)
    p_rows = p.reshape(depth, n, -1)
    cos, sin = _rope_tables(positions)
    gain = lambda g, i: g[i].reshape(1, -1).astype(F32)
    (ffn1_w_gu, ffn1_w_down, w_in, w_branch_sb, w_branch_diff, w_out, ffn2_w_gu, ffn2_w_down,
     w_ple_gate, w_ple_proj) = (w.astype(BF16) for w in (
         ffn1_w_gu, ffn1_w_down, w_in, w_branch_sb, w_branch_diff, w_out, ffn2_w_gu, ffn2_w_down,
         w_ple_gate, w_ple_proj))
    for i in range(depth):
        h = _ffn(h, gain(ffn1_pre_g, i), ffn1_w_gu, ffn1_w_down, gain(ffn1_post_g, i), i)
        sbq, sbk, sbv, dfq, dfk, dfv, gates = _proj(h, gain(mix_pre_g, i), w_in, i, cos, sin)
        o_sb = _sb_attention(sbq, sbk, sbv, b, s)
        lam_init = 0.8 - 0.6 * math.exp(-0.3 * i)
        o_df = _diff_attention(diff_lambda[i].astype(F32), gain(diff_subln_g, i), dfq, dfk, dfv, b, s, lam_init)
        h = _merge(h, o_sb, o_df, gates, w_branch_sb, w_branch_diff, w_out, i, gain(mix_post_g, i))
        ple = (p_rows, gain(ple_pre_g, i), w_ple_gate, w_ple_proj, gain(ple_post_g, i))
        h = _ffn(h, gain(ffn2_pre_g, i), ffn2_w_gu, ffn2_w_down, gain(ffn2_post_g, i), i, ple=ple)
    return h.reshape(b, s, d)
```
